```python
import math
import jax, jax.numpy as jnp
from jax import lax
import numpy as np

D_MODEL = 1024
BATCH = 4
SEQ = 8192
DEPTH = 2

N_A_LAYERS = DEPTH // 2
N_B_LAYERS = DEPTH - N_A_LAYERS

GDN_HEADS = 8
GDN_DK = 128
GDN_DV = 128
GDN_KDIM = GDN_HEADS * GDN_DK
GDN_VDIM = GDN_HEADS * GDN_DV
QKV_DIM = 2 * GDN_KDIM + GDN_VDIM
GDN_PROJ_DIM = QKV_DIM + GDN_VDIM + 2 * GDN_HEADS
CONV_WIDTH = 4
CHUNK = 64

SB_HEADS = 8
SB_DH = D_MODEL // SB_HEADS
SB_DIM = SB_HEADS * SB_DH
Q_BLOCK = 128

D_FF = -(-8 * D_MODEL // (3 * 256)) * 256

EPS = 1e-6

kernel_name = "yoco_gated_deltanet_stick_breaking"


def rms_norm(x, gain):
    xf = x.astype(jnp.float32)
    y = xf * lax.rsqrt(jnp.mean(xf * xf, axis=-1, keepdims=True) + EPS)
    return (y * gain.astype(jnp.float32)).astype(x.dtype)


def l2_norm(x):
    xf = x.astype(jnp.float32)
    return xf * lax.rsqrt(jnp.sum(xf * xf, axis=-1, keepdims=True) + EPS)


def causal_depthwise_conv(x, w):
    c = x.shape[-1]
    return lax.conv_general_dilated(
        x, w[:, None, :].astype(x.dtype), window_strides=(1,),
        padding=[(CONV_WIDTH - 1, 0)], dimension_numbers=('NWC', 'WIO', 'NWC'),
        feature_group_count=c)


def swiglu(x, w_gu, w_down):
    g, u = jnp.split(x @ w_gu, 2, axis=-1)
    return (jax.nn.silu(g) * u) @ w_down


def gated_delta_rule(q, k, v, beta, g):
    b_, h_, s_, dk = q.shape
    dv = v.shape[-1]
    n = s_ // CHUNK
    q = (q * dk ** -0.5).reshape(b_, h_, n, CHUNK, dk)
    k = k.reshape(b_, h_, n, CHUNK, dk)
    v = v.reshape(b_, h_, n, CHUNK, dv)
    beta = beta.reshape(b_, h_, n, CHUNK)
    g_cum = jnp.cumsum(g.reshape(b_, h_, n, CHUNK), axis=-1)

    idx = jnp.arange(CHUNK)
    incl = idx[:, None] >= idx[None, :]
    strict = idx[:, None] > idx[None, :]
    decay = jnp.exp(jnp.where(incl, g_cum[..., :, None] - g_cum[..., None, :], -jnp.inf))

    k_beta = k * beta[..., None]
    a_low = jnp.where(strict, jnp.einsum('bhncd,bhnsd->bhncs', k_beta, k) * decay, 0.0)
    eye = jnp.eye(CHUNK, dtype=q.dtype)
    rhs = jnp.concatenate([v * beta[..., None], k_beta * jnp.exp(g_cum)[..., None]], axis=-1)
    sol = lax.linalg.triangular_solve(eye + a_low, rhs, left_side=True, lower=True,
                                      unit_diagonal=True)
    u, w = sol[..., :dv], sol[..., dv:]

    attn_intra = jnp.einsum('bhncd,bhnsd->bhncs', q, k) * decay
    g_last = g_cum[..., -1]
    k_decay = k * jnp.exp(g_last[..., None] - g_cum)[..., None]
    q_decay = q * jnp.exp(g_cum)[..., None]

    def step(state, inp):
        qd, kd, ui, wi, ai, gl = inp
        v_new = ui - jnp.einsum('bhck,bhkv->bhcv', wi, state)
        o = (jnp.einsum('bhck,bhkv->bhcv', qd, state)
             + jnp.einsum('bhcs,bhsv->bhcv', ai, v_new))
        state = state * jnp.exp(gl)[..., None, None] + jnp.einsum('bhck,bhcv->bhkv', kd, v_new)
        return state, o

    xs = tuple(jnp.moveaxis(t, 2, 0) for t in (q_decay, k_decay, u, w, attn_intra, g_last))
    state0 = jnp.zeros((b_, h_, dk, dv), q.dtype)
    _, o = lax.scan(step, state0, xs)
    return jnp.moveaxis(o, 0, 2).reshape(b_, h_, s_, dv)


def gdn_mixer(hn, w_in, conv_w, a_log, dt_bias, o_gain, w_out):
    b_, s_, _ = hn.shape
    f32 = jnp.float32
    proj = hn @ w_in
    qkv, z, b_raw, a_raw = jnp.split(
        proj, [QKV_DIM, QKV_DIM + GDN_VDIM, QKV_DIM + GDN_VDIM + GDN_HEADS], axis=-1)
    qkv = jax.nn.silu(causal_depthwise_conv(qkv, conv_w))
    q, k, v = jnp.split(qkv, [GDN_KDIM, 2 * GDN_KDIM], axis=-1)

    def heads(t, d):
        return t.reshape(b_, s_, GDN_HEADS, d).transpose(0, 2, 1, 3)

    q = l2_norm(heads(q, GDN_DK))
    k = l2_norm(heads(k, GDN_DK))
    v = heads(v, GDN_DV).astype(f32)
    beta = jax.nn.sigmoid(b_raw.astype(f32)).transpose(0, 2, 1)
    g = (-jnp.exp(a_log.astype(f32))
         * jax.nn.softplus(a_raw.astype(f32) + dt_bias.astype(f32))).transpose(0, 2, 1)
    o = gated_delta_rule(q, k, v, beta, g).transpose(0, 2, 1, 3)
    zf = z.reshape(b_, s_, GDN_HEADS, GDN_DV).astype(f32)
    o = rms_norm(o, o_gain) * jax.nn.silu(zf)
    return o.reshape(b_, s_, GDN_VDIM).astype(hn.dtype) @ w_out


def shared_kv(h, kv_norm, w_kv, k_gain):
    b_, s_, _ = h.shape
    k, v = jnp.split(rms_norm(h, kv_norm) @ w_kv, 2, axis=-1)
    k = rms_norm(k.reshape(b_, s_, SB_HEADS, SB_DH), k_gain).transpose(0, 2, 1, 3)
    v = v.reshape(b_, s_, SB_HEADS, SB_DH).transpose(0, 2, 1, 3)
    return k, v


def stick_breaking_attention(q, k, v):
    _, _, s_, dh = q.shape
    f32 = jnp.float32
    scale = dh ** -0.5
    outs = []
    for blk in range(s_ // Q_BLOCK):
        t0 = blk * Q_BLOCK
        end = t0 + Q_BLOCK
        qb = q[:, :, t0:end].astype(f32)
        kb = k[:, :, :end].astype(f32)
        vb = v[:, :, :end].astype(f32)
        z = jnp.einsum('bhtd,bhsd->bhts', qb, kb) * scale
        t_idx = t0 + jnp.arange(Q_BLOCK)
        s_idx = jnp.arange(end)
        causal = s_idx[None, :] < t_idx[:, None]
        ls_neg = jax.nn.log_sigmoid(-z)
        log_fail = jnp.where(causal, ls_neg, 0.0)
        log_rest = lax.cumsum(log_fail, axis=3, reverse=True) - log_fail
        log_a = z + ls_neg + log_rest
        att = jnp.where(causal, jnp.exp(log_a), 0.0)
        outs.append(jnp.einsum('bhts,bhsd->bhtd', att, vb))
    return jnp.concatenate(outs, axis=2).astype(q.dtype)


def sb_mixer(hn, k_sh, v_sh, w_q, q_gain, w_out):
    b_, s_, _ = hn.shape
    q = rms_norm((hn @ w_q).reshape(b_, s_, SB_HEADS, SB_DH), q_gain).transpose(0, 2, 1, 3)
    o = stick_breaking_attention(q, k_sh, v_sh)
    return o.transpose(0, 2, 1, 3).reshape(b_, s_, SB_DIM).astype(hn.dtype) @ w_out


def setup_inputs(seed: int = 0) -> dict:
    key = jax.random.key(seed)
    ks = jax.random.split(key, 17)
    f32 = jnp.float32

    def dense(k, shape, fan_in, scale=1.0):
        return jax.random.normal(k, shape, f32) * (scale * fan_in ** -0.5)

    def gain(k, shape):
        return 1.0 + 0.02 * jax.random.normal(k, shape, f32)

    out_scale = (2 * DEPTH) ** -0.5
    x = jax.random.normal(ks[0], (BATCH, SEQ, D_MODEL), f32)
    attn_norm = gain(ks[1], (DEPTH, D_MODEL))
    ffn_norm = gain(ks[2], (DEPTH, D_MODEL))
    gdn_w_in = dense(ks[3], (N_A_LAYERS, D_MODEL, GDN_PROJ_DIM), D_MODEL)
    gdn_conv_w = jax.random.normal(ks[4], (N_A_LAYERS, CONV_WIDTH, QKV_DIM), f32) * CONV_WIDTH ** -0.5
    gdn_a_log = jnp.log(jax.random.uniform(ks[5], (N_A_LAYERS, GDN_HEADS), f32, 1.0, 16.0))
    dt = jnp.exp(jax.random.uniform(ks[6], (N_A_LAYERS, GDN_HEADS), f32,
                                    math.log(1e-3), math.log(1e-1)))
    gdn_dt_bias = dt + jnp.log(-jnp.expm1(-dt))
    gdn_o_gain = gain(ks[7], (N_A_LAYERS, GDN_DV))
    gdn_w_out = dense(ks[8], (N_A_LAYERS, GDN_VDIM, D_MODEL), GDN_VDIM, out_scale)
    kv_norm = gain(ks[9], (D_MODEL,))
    w_kv = dense(ks[10], (D_MODEL, 2 * SB_DIM), D_MODEL)
    k_gain = gain(ks[11], (SB_DH,))
    sb_w_q = dense(ks[12], (N_B_LAYERS, D_MODEL, SB_DIM), D_MODEL)
    sb_q_gain = gain(ks[13], (N_B_LAYERS, SB_DH))
    sb_w_out = dense(ks[14], (N_B_LAYERS, SB_DIM, D_MODEL), SB_DIM, out_scale)
    ffn_w_gu = dense(ks[15], (DEPTH, D_MODEL, 2 * D_FF), D_MODEL)
    ffn_w_down = dense(ks[16], (DEPTH, D_FF, D_MODEL), D_FF, out_scale)
    return {"x": x, "attn_norm": attn_norm, "ffn_norm": ffn_norm,
            "gdn_w_in": gdn_w_in, "gdn_conv_w": gdn_conv_w, "gdn_a_log": gdn_a_log,
            "gdn_dt_bias": gdn_dt_bias, "gdn_o_gain": gdn_o_gain, "gdn_w_out": gdn_w_out,
            "kv_norm": kv_norm, "w_kv": w_kv, "k_gain": k_gain,
            "sb_w_q": sb_w_q, "sb_q_gain": sb_q_gain, "sb_w_out": sb_w_out,
            "ffn_w_gu": ffn_w_gu, "ffn_w_down": ffn_w_down}


def reference(x, attn_norm, ffn_norm, gdn_w_in, gdn_conv_w, gdn_a_log, gdn_dt_bias,
              gdn_o_gain, gdn_w_out, kv_norm, w_kv, k_gain, sb_w_q, sb_q_gain, sb_w_out,
              ffn_w_gu, ffn_w_down):
    h = x
    k_sh = None
    v_sh = None
    for layer in range(DEPTH):
        hn = rms_norm(h, attn_norm[layer])
        if layer < N_A_LAYERS:
            i = layer
            h = h + gdn_mixer(hn, gdn_w_in[i], gdn_conv_w[i], gdn_a_log[i], gdn_dt_bias[i],
                              gdn_o_gain[i], gdn_w_out[i])
        else:
            if layer == N_A_LAYERS:
                k_sh, v_sh = shared_kv(h, kv_norm, w_kv, k_gain)
            j = layer - N_A_LAYERS
            h = h + sb_mixer(hn, k_sh, v_sh, sb_w_q[j], sb_q_gain[j], sb_w_out[j])
        h = h + swiglu(rms_norm(h, ffn_norm[layer]), ffn_w_gu[layer], ffn_w_down[layer])
    return h
```

```python
import functools

import jax
import jax.numpy as jnp
from jax import lax
from jax.experimental import pallas as pl
from jax.experimental.pallas import tpu as pltpu

F32 = jnp.float32
BF16 = jnp.bfloat16
EPS = 1e-6

LANES = 128
GDN_HEADS = 8
GDN_D = 128
CHUNK = 64
PAIR = 2 * CHUNK
CONV_WIDTH = 4
CONV_HALO = 8
SB_HEADS = 8
SB_DH = 128
VMEM_LIMIT = 56 * 1024 * 1024


def _dot(a, b):
    return jnp.dot(a, b, preferred_element_type=F32)


def _dot_nt(a, b):
    return lax.dot_general(a, b, (((1,), (1,)), ((), ())), preferred_element_type=F32)


def _silu(x):
    return x * (1.0 / (1.0 + jnp.exp(-x)))


def _softplus(x):
    return jnp.maximum(x, 0.0) + jnp.log(1.0 + jnp.exp(-jnp.abs(x)))


def _split_bf16(x):
    hi = x.astype(BF16)
    lo = (x - hi.astype(F32)).astype(BF16)
    return hi, lo


def _rms_scale(x):
    return lax.rsqrt(jnp.mean(x * x, axis=-1, keepdims=True) + EPS)


def _inproj_kernel(x_ref, g_ref, w_ref, wba_hi_ref, wba_lo_ref, y_ref, ba_ref, xn_ref):
    @pl.when(pl.program_id(1) == 0)
    def _():
        x = x_ref[...]
        xn = x * _rms_scale(x) * g_ref[...]
        xh, xl = _split_bf16(xn)
        xn_ref[...] = xh
        ba_ref[...] = (_dot(xh, wba_hi_ref[...]) + _dot(xl, wba_hi_ref[...])
                       + _dot(xh, wba_lo_ref[...]))

    y_ref[...] = _dot(xn_ref[...], w_ref[...]).astype(y_ref.dtype)


def _gdn_inproj(x2, gain, w_main, wba_hi, wba_lo, *, tm, tn, out_dtype):
    t, d = x2.shape
    n = w_main.shape[1]
    return pl.pallas_call(
        _inproj_kernel,
        out_shape=(jax.ShapeDtypeStruct((t, n), out_dtype),
                   jax.ShapeDtypeStruct((t, LANES), F32)),
        grid=(t // tm, n // tn),
        in_specs=[
            pl.BlockSpec((tm, d), lambda i, j: (i, 0)),
            pl.BlockSpec((1, d), lambda i, j: (0, 0)),
            pl.BlockSpec((d, tn), lambda i, j: (0, j)),
            pl.BlockSpec((d, LANES), lambda i, j: (0, 0)),
            pl.BlockSpec((d, LANES), lambda i, j: (0, 0)),
        ],
        out_specs=(pl.BlockSpec((tm, tn), lambda i, j: (i, j)),
                   pl.BlockSpec((tm, LANES), lambda i, j: (i, 0))),
        scratch_shapes=[pltpu.VMEM((tm, d), BF16)],
        compiler_params=pltpu.CompilerParams(
            dimension_semantics=("parallel", "arbitrary"), vmem_limit_bytes=VMEM_LIMIT),
        name="gdn_inproj",
    )(x2, gain, w_main, wba_hi, wba_lo)


def _tri_inverse(a, row, col):
    eye = jnp.where(row == col, 1.0, 0.0)
    t = eye - jnp.where((row - col == 1) & ((row & 1) == 1), a, 0.0)
    s = 2
    while s < CHUNK:
        sub = ((row & -(2 * s)) == (col & -(2 * s))) & ((row & s) != 0) & ((col & s) == 0)
        am = jnp.where(sub, a, 0.0).astype(BF16)
        tb = t.astype(BF16)
        t = t - _dot(tb, _dot(am, tb).astype(BF16))
        s *= 2
    return t


def _gdn_kernel(q_ref, k_ref, v_ref, z_ref, ba_ref, cwq_ref, cwk_ref, cwv_ref,
                alog_ref, dtb_ref, og_ref, o_ref, state_ref, ext_ref, *, blk):
    h = pl.program_id(1)
    first = pl.program_id(2) == 0

    @pl.when(first)
    def _():
        state_ref[...] = jnp.zeros_like(state_ref)
        ext_ref[:, 0:CONV_HALO, :] = jnp.zeros((3, CONV_HALO, GDN_D), F32)

    def conv_silu(idx, x_ref, cw_ref):
        ext_ref[idx, CONV_HALO:CONV_HALO + blk, :] = x_ref[...].astype(F32)
        y = cw_ref[CONV_WIDTH - 1:CONV_WIDTH, :] * ext_ref[idx, CONV_HALO:CONV_HALO + blk, :]
        for j in range(CONV_WIDTH - 1):
            off = CONV_HALO - (CONV_WIDTH - 1) + j
            y = y + cw_ref[j:j + 1, :] * ext_ref[idx, off:off + blk, :]
        ext_ref[idx, 0:CONV_HALO, :] = ext_ref[idx, blk:blk + CONV_HALO, :]
        return _silu(y)

    q = conv_silu(0, q_ref, cwq_ref)
    k = conv_silu(1, k_ref, cwk_ref)
    v = conv_silu(2, v_ref, cwv_ref)
    q = q * (lax.rsqrt(jnp.sum(q * q, axis=-1, keepdims=True) + EPS) * (GDN_D ** -0.5))
    k = k * lax.rsqrt(jnp.sum(k * k, axis=-1, keepdims=True) + EPS)

    ba = ba_ref[...]
    lane = lax.broadcasted_iota(jnp.int32, ba.shape, 1)
    beta_all = 1.0 / (1.0 + jnp.exp(-ba))
    g_all = -jnp.exp(alog_ref[...]) * _softplus(ba + dtb_ref[...])
    beta = jnp.sum(jnp.where(lane == h, beta_all, 0.0), axis=-1, keepdims=True)
    g = jnp.sum(jnp.where(lane == h + GDN_HEADS, g_all, 0.0), axis=-1, keepdims=True)

    row = lax.broadcasted_iota(jnp.int32, (PAIR, PAIR), 0)
    col = lax.broadcasted_iota(jnp.int32, (PAIR, PAIR), 1)
    same = (row & -CHUNK) == (col & -CHUNK)
    incl = same & (row >= col)
    strict = same & (row > col)
    cum_mat = jnp.where(incl, 1.0, 0.0).astype(BF16)

    og = og_ref[...]
    for p in range(blk // PAIR):
        r0 = p * PAIR
        qp, kp, vp = q[r0:r0 + PAIR], k[r0:r0 + PAIR], v[r0:r0 + PAIR]
        bp = beta[r0:r0 + PAIR]
        g_hi, g_lo = _split_bf16(jnp.broadcast_to(g[r0:r0 + PAIR], (PAIR, PAIR)))
        gc = _dot(cum_mat, g_hi) + _dot(cum_mat, g_lo)
        gc_t = gc.T
        decay = jnp.exp(jnp.where(incl, gc - gc_t, -jnp.inf))
        kb = kp * bp
        kbf = kp.astype(BF16)
        a_low = jnp.where(strict, _dot_nt(kb.astype(BF16), kbf) * decay, 0.0)
        attn = _dot_nt(qp.astype(BF16), kbf) * decay
        t_inv = _tri_inverse(a_low, row, col)
        eg = jnp.exp(gc)
        rhs = jnp.concatenate([vp * bp, kb * eg], axis=-1).astype(BF16)
        sol = _dot(t_inv.astype(BF16), rhs)
        u, w = sol[:, :GDN_D], sol[:, GDN_D:]
        qd = (qp * eg).astype(BF16)
        wb = w.astype(BF16)

        g_last = [gc[c * CHUNK + CHUNK - 1:(c + 1) * CHUNK, :] for c in range(2)]
        kd_t = (kp * jnp.exp(jnp.where(row < CHUNK, g_last[0], g_last[1]) - gc)).T

        s = state_ref[...]
        vns, inter = [], []
        for c in range(2):
            c0 = c * CHUNK
            s_bf = s.astype(BF16)
            vn = u[c0:c0 + CHUNK] - _dot(wb[c0:c0 + CHUNK], s_bf)
            inter.append(_dot(qd[c0:c0 + CHUNK], s_bf))
            vns.append(vn)
            zeros = jnp.zeros_like(vn)
            vn_full = jnp.concatenate([vn, zeros] if c == 0 else [zeros, vn], axis=0).astype(BF16)
            kd_c = jnp.where((col & CHUNK) == c * CHUNK, kd_t, 0.0).astype(BF16)
            s = s * jnp.exp(g_last[c]) + _dot(kd_c, vn_full)
        state_ref[...] = s

        vn_all = jnp.concatenate(vns, axis=0).astype(BF16)
        o = jnp.concatenate(inter, axis=0) + _dot(attn.astype(BF16), vn_all)
        zg = z_ref[r0:r0 + PAIR, :].astype(F32)
        o = o * _rms_scale(o) * og * _silu(zg)
        o_ref[r0:r0 + PAIR, :] = o.astype(o_ref.dtype)


def _gdn_mixer(proj3, ba3, conv_w8, alog_pad, dtb_pad, o_gain, *, blk):
    b, s, _ = proj3.shape
    h = GDN_HEADS
    tok = lambda off: pl.BlockSpec((None, blk, GDN_D), lambda bi, hi, li: (bi, li, hi + off))
    cw = lambda off: pl.BlockSpec((8, GDN_D), lambda bi, hi, li: (0, hi + off))
    vec = pl.BlockSpec((1, LANES), lambda bi, hi, li: (0, 0))
    return pl.pallas_call(
        functools.partial(_gdn_kernel, blk=blk),
        out_shape=jax.ShapeDtypeStruct((b, s, h * GDN_D), BF16),
        grid=(b, h, s // blk),
        in_specs=[tok(0), tok(h), tok(2 * h), tok(3 * h),
                  pl.BlockSpec((None, blk, LANES), lambda bi, hi, li: (bi, li, 0)),
                  cw(0), cw(h), cw(2 * h), vec, vec, vec],
        out_specs=pl.BlockSpec((None, blk, GDN_D), lambda bi, hi, li: (bi, li, hi)),
        scratch_shapes=[pltpu.VMEM((GDN_D, GDN_D), F32),
                        pltpu.VMEM((3, blk + CONV_HALO, GDN_D), F32)],
        compiler_params=pltpu.CompilerParams(
            dimension_semantics=("parallel", "parallel", "arbitrary"), vmem_limit_bytes=VMEM_LIMIT),
        name="gdn_mixer",
    )(proj3, proj3, proj3, proj3, ba3, conv_w8, conv_w8, conv_w8, alog_pad, dtb_pad, o_gain)


def _proj_res_kernel(a_ref, w_ref, r_ref, o_ref):
    o_ref[...] = r_ref[...] + _dot(a_ref[...], w_ref[...])


def _proj_res(a2, w, res2, *, tm, tn):
    t, kdim = a2.shape
    n = w.shape[1]
    return pl.pallas_call(
        _proj_res_kernel,
        out_shape=jax.ShapeDtypeStruct((t, n), F32),
        grid=(t // tm, n // tn),
        in_specs=[pl.BlockSpec((tm, kdim), lambda i, j: (i, 0)),
                  pl.BlockSpec((kdim, tn), lambda i, j: (0, j)),
                  pl.BlockSpec((tm, tn), lambda i, j: (i, j))],
        out_specs=pl.BlockSpec((tm, tn), lambda i, j: (i, j)),
        compiler_params=pltpu.CompilerParams(
            dimension_semantics=("parallel", "parallel"), vmem_limit_bytes=VMEM_LIMIT),
        name="proj_res",
    )(a2, w, res2)


def _ffn_kernel(h_ref, g_ref, wg_ref, wu_ref, wd_ref, o_ref, xn_ref, acc_ref):
    j = pl.program_id(1)

    @pl.when(j == 0)
    def _():
        x = h_ref[...]
        xn_ref[...] = (x * _rms_scale(x) * g_ref[...]).astype(BF16)
        acc_ref[...] = jnp.zeros_like(acc_ref)

    xn = xn_ref[...]
    gate = _dot(xn, wg_ref[...])
    up = _dot(xn, wu_ref[...])
    acc_ref[...] += _dot((_silu(gate) * up).astype(BF16), wd_ref[...])

    @pl.when(j == pl.num_programs(1) - 1)
    def _():
        o_ref[...] = h_ref[...] + acc_ref[...]


def _ffn(h2, gain, w_gu, w_down, *, tm, tf):
    t, d = h2.shape
    f = w_down.shape[0]
    nf = f // tf
    return pl.pallas_call(
        _ffn_kernel,
        out_shape=jax.ShapeDtypeStruct((t, d), F32),
        grid=(t // tm, nf),
        in_specs=[pl.BlockSpec((tm, d), lambda i, j: (i, 0)),
                  pl.BlockSpec((1, d), lambda i, j: (0, 0)),
                  pl.BlockSpec((d, tf), lambda i, j: (0, j)),
                  pl.BlockSpec((d, tf), lambda i, j: (0, j + nf)),
                  pl.BlockSpec((tf, d), lambda i, j: (j, 0))],
        out_specs=pl.BlockSpec((tm, d), lambda i, j: (i, 0)),
        scratch_shapes=[pltpu.VMEM((tm, d), BF16), pltpu.VMEM((tm, d), F32)],
        compiler_params=pltpu.CompilerParams(
            dimension_semantics=("parallel", "arbitrary"), vmem_limit_bytes=VMEM_LIMIT),
        name="ffn",
    )(h2, gain, w_gu, w_gu, w_down)


def _sb_qkv_kernel(h_ref, gq_ref, gkv_ref, w_ref, qg_ref, kg_ref, o_ref, xq_ref, xkv_ref, *, nq, nk):
    j = pl.program_id(1)

    @pl.when(j == 0)
    def _():
        x = h_ref[...]
        xs = x * _rms_scale(x)
        xq_ref[...] = (xs * gq_ref[...]).astype(BF16)
        xkv_ref[...] = (xs * gkv_ref[...]).astype(BF16)

    def head_norm(y, gain):
        for c in range(y.shape[1] // SB_DH):
            yc = y[:, c * SB_DH:(c + 1) * SB_DH]
            o_ref[:, c * SB_DH:(c + 1) * SB_DH] = (yc * _rms_scale(yc) * gain).astype(o_ref.dtype)

    @pl.when(j < nq)
    def _():
        head_norm(_dot(xq_ref[...], w_ref[...]), qg_ref[...] * (SB_DH ** -0.5))

    @pl.when((j >= nq) & (j < nq + nk))
    def _():
        head_norm(_dot(xkv_ref[...], w_ref[...]), kg_ref[...])

    @pl.when(j >= nq + nk)
    def _():
        o_ref[...] = _dot(xkv_ref[...], w_ref[...]).astype(o_ref.dtype)


def _sb_qkv(h2, gain_q, gain_kv, w_qkv, q_gain, k_gain, *, tm, tn):
    t, d = h2.shape
    n = w_qkv.shape[1]
    nq = (SB_HEADS * SB_DH) // tn
    vec = lambda width: pl.BlockSpec((1, width), lambda i, j: (0, 0))
    return pl.pallas_call(
        functools.partial(_sb_qkv_kernel, nq=nq, nk=nq),
        out_shape=jax.ShapeDtypeStruct((t, n), BF16),
        grid=(t // tm, n // tn),
        in_specs=[pl.BlockSpec((tm, d), lambda i, j: (i, 0)), vec(d), vec(d),
                  pl.BlockSpec((d, tn), lambda i, j: (0, j)), vec(SB_DH), vec(SB_DH)],
        out_specs=pl.BlockSpec((tm, tn), lambda i, j: (i, j)),
        scratch_shapes=[pltpu.VMEM((tm, d), BF16), pltpu.VMEM((tm, d), BF16)],
        compiler_params=pltpu.CompilerParams(
            dimension_semantics=("parallel", "arbitrary"), vmem_limit_bytes=VMEM_LIMIT),
        name="sb_qkv",
    )(h2, gain_q, gain_kv, w_qkv, q_gain, k_gain)


def _sb_attn_kernel(q_ref, k_ref, v_ref, suf_ref, o_ref, *, tq):
    qi = pl.program_id(2)
    q = q_ref[...]
    suf = suf_ref[...]

    def key_block(j, rest, acc, causal):
        start = pl.multiple_of(j * tq, tq)
        z = _dot_nt(q, k_ref[pl.ds(start, tq), :])
        ls_neg = -_softplus(z)
        log_fail = ls_neg if causal is None else jnp.where(causal, ls_neg, 0.0)
        hi, lo = _split_bf16(log_fail)
        log_rest = _dot(hi, suf) + _dot(lo, suf) + rest
        att = jnp.exp(z + ls_neg + log_rest)
        if causal is not None:
            att = jnp.where(causal, att, 0.0)
        acc = acc + _dot(att.astype(BF16), v_ref[pl.ds(start, tq), :])
        rest = rest + jnp.sum(log_fail, axis=-1, keepdims=True)
        return rest, acc

    row = lax.broadcasted_iota(jnp.int32, (tq, tq), 0)
    col = lax.broadcasted_iota(jnp.int32, (tq, tq), 1)
    rest, acc = key_block(qi, jnp.zeros((tq, 1), F32), jnp.zeros((tq, SB_DH), F32), col < row)

    def body(it, carry):
        return key_block(qi - 1 - it, carry[0], carry[1], None)

    rest, acc = lax.fori_loop(0, qi, body, (rest, acc))
    o_ref[...] = acc.astype(o_ref.dtype)


def _sb_attn(qkv3, suffix_mat, *, tq):
    b, s, _ = qkv3.shape
    h = SB_HEADS
    return pl.pallas_call(
        functools.partial(_sb_attn_kernel, tq=tq),
        out_shape=jax.ShapeDtypeStruct((b, s, h * SB_DH), BF16),
        grid=(b, h, s // tq),
        in_specs=[pl.BlockSpec((None, tq, SB_DH), lambda bi, hi, qi: (bi, qi, hi)),
                  pl.BlockSpec((None, s, SB_DH), lambda bi, hi, qi: (bi, 0, hi + h)),
                  pl.BlockSpec((None, s, SB_DH), lambda bi, hi, qi: (bi, 0, hi + 2 * h)),
                  pl.BlockSpec((tq, tq), lambda bi, hi, qi: (0, 0))],
        out_specs=pl.BlockSpec((None, tq, SB_DH), lambda bi, hi, qi: (bi, qi, hi)),
        compiler_params=pltpu.CompilerParams(
            dimension_semantics=("parallel", "parallel", "arbitrary"), vmem_limit_bytes=VMEM_LIMIT),
        name="sb_attn",
    )(qkv3, qkv3, qkv3, suffix_mat)


def _pick(n, prefs):
    for p in prefs:
        if n % p == 0:
            return p
    return n


def kernel(x, attn_norm, ffn_norm, gdn_w_in, gdn_conv_w, gdn_a_log, gdn_dt_bias, gdn_o_gain, gdn_w_out,
           kv_norm, w_kv, k_gain, sb_w_q, sb_q_gain, sb_w_out, ffn_w_gu, ffn_w_down):
    b, s, d = x.shape
    t = b * s
    hd = GDN_HEADS * GDN_D
    n_main = 4 * hd
    tm = _pick(t, (512, 256, 128))
    tm_ffn = _pick(t, (1024, 512, 256, 128))
    gdn_blk = _pick(s, (256, 128))
    tq = _pick(s, (256, 128))

    row = lambda vec: vec.reshape(1, -1).astype(F32)
    x2 = x.reshape(t, d)

    w_in = gdn_w_in[0]
    w_ba = jnp.pad(w_in[:, n_main:], ((0, 0), (0, LANES - 2 * GDN_HEADS)))
    wba_hi = w_ba.astype(BF16)
    wba_lo = (w_ba - wba_hi.astype(F32)).astype(BF16)
    proj, ba = _gdn_inproj(x2, row(attn_norm[0]), w_in[:, :n_main].astype(BF16), wba_hi, wba_lo,
                           tm=tm, tn=512, out_dtype=F32)
    lane_pad = lambda vec: jnp.pad(vec.astype(F32), (GDN_HEADS, LANES - 2 * GDN_HEADS)).reshape(1, LANES)
    conv_w8 = jnp.pad(gdn_conv_w[0].astype(F32), ((0, 8 - CONV_WIDTH), (0, 0)))
    o_gdn = _gdn_mixer(proj.reshape(b, s, n_main), ba.reshape(b, s, LANES), conv_w8,
                       lane_pad(gdn_a_log[0]), lane_pad(gdn_dt_bias[0]), row(gdn_o_gain[0]), blk=gdn_blk)
    h2 = _proj_res(o_gdn.reshape(t, hd), gdn_w_out[0].astype(BF16), x2, tm=tm, tn=512)
    h2 = _ffn(h2, row(ffn_norm[0]), ffn_w_gu[0].astype(BF16), ffn_w_down[0].astype(BF16),
              tm=tm_ffn, tf=256)

    w_qkv = jnp.concatenate([sb_w_q[0], w_kv], axis=1).astype(BF16)
    qkv = _sb_qkv(h2, row(attn_norm[1]), row(kv_norm), w_qkv, row(sb_q_gain[0]), row(k_gain),
                  tm=tm, tn=512)
    idx = jnp.arange(tq)
    suffix_mat = (idx[:, None] > idx[None, :]).astype(BF16)
    o_sb = _sb_attn(qkv.reshape(b, s, 3 * SB_HEADS * SB_DH), suffix_mat, tq=tq)
    h2 = _proj_res(o_sb.reshape(t, SB_HEADS * SB_DH), sb_w_out[0].astype(BF16), h2, tm=tm, tn=512)
    h2 = _ffn(h2, row(ffn_norm[1]), ffn_w_gu[1].astype(BF16), ffn_w_down[1].astype(BF16),
              tm=tm_ffn, tf=256)
    return h2.reshape(b, s, d)
```

```python
import functools

import jax
import jax.numpy as jnp
from jax import lax
from jax.experimental import pallas as pl
from jax.experimental.pallas import tpu as pltpu

F32 = jnp.float32
BF16 = jnp.bfloat16
EPS = 1e-6
LOG2E = 1.4426950408889634

LANES = 128
GDN_HEADS = 8
GDN_D = 128
CHUNK = 64
PAIR = 2 * CHUNK
CONV_WIDTH = 4
CONV_HALO = 8
SB_HEADS = 8
SB_DH = 128
VMEM_LIMIT = 56 * 1024 * 1024


def _dot(a, b):
    return jnp.dot(a, b, preferred_element_type=F32)


def _dot_nt(a, b):
    return lax.dot_general(a, b, (((1,), (1,)), ((), ())), preferred_element_type=F32)


def _silu(x):
    return x * (1.0 / (1.0 + jnp.exp(-x)))


def _softplus(x):
    return jnp.maximum(x, 0.0) + jnp.log(1.0 + jnp.exp(-jnp.abs(x)))


def _split_bf16(x):
    hi = x.astype(BF16)
    lo = (x - hi.astype(F32)).astype(BF16)
    return hi, lo


def _rms_scale(x):
    return lax.rsqrt(jnp.mean(x * x, axis=-1, keepdims=True) + EPS)


def _inproj_kernel(x_ref, g_ref, w_ref, wba_hi_ref, wba_lo_ref, y_ref, ba_ref, xn_ref):
    @pl.when(pl.program_id(1) == 0)
    def _():
        x = x_ref[...]
        xn = x * _rms_scale(x) * g_ref[...]
        xh, xl = _split_bf16(xn)
        xn_ref[...] = xh
        ba_ref[...] = (_dot(xh, wba_hi_ref[...]) + _dot(xl, wba_hi_ref[...])
                       + _dot(xh, wba_lo_ref[...]))

    y_ref[...] = _dot(xn_ref[...], w_ref[...]).astype(y_ref.dtype)


def _gdn_inproj(x2, gain, w_main, wba_hi, wba_lo, *, tm, tn, out_dtype):
    t, d = x2.shape
    n = w_main.shape[1]
    return pl.pallas_call(
        _inproj_kernel,
        out_shape=(jax.ShapeDtypeStruct((t, n), out_dtype),
                   jax.ShapeDtypeStruct((t, LANES), F32)),
        grid=(t // tm, n // tn),
        in_specs=[
            pl.BlockSpec((tm, d), lambda i, j: (i, 0)),
            pl.BlockSpec((1, d), lambda i, j: (0, 0)),
            pl.BlockSpec((d, tn), lambda i, j: (0, j)),
            pl.BlockSpec((d, LANES), lambda i, j: (0, 0)),
            pl.BlockSpec((d, LANES), lambda i, j: (0, 0)),
        ],
        out_specs=(pl.BlockSpec((tm, tn), lambda i, j: (i, j)),
                   pl.BlockSpec((tm, LANES), lambda i, j: (i, 0))),
        scratch_shapes=[pltpu.VMEM((tm, d), BF16)],
        compiler_params=pltpu.CompilerParams(
            dimension_semantics=("parallel", "arbitrary"), vmem_limit_bytes=VMEM_LIMIT),
        name="gdn_inproj",
    )(x2, gain, w_main, wba_hi, wba_lo)


def _gdn_kernel(q_ref, k_ref, v_ref, z_ref, ba_ref, cwq_ref, cwk_ref, cwv_ref,
                alog_ref, dtb_ref, og_ref, o_ref, state_ref, ext_ref, *, blk, hb):
    hg = pl.program_id(1)
    width = hb * GDN_D
    npair = blk // PAIR

    @pl.when(pl.program_id(2) == 0)
    def _():
        state_ref[...] = jnp.zeros_like(state_ref)
        ext_ref[:, 0:CONV_HALO, :] = jnp.zeros((3, CONV_HALO, width), F32)

    def conv_silu(idx, x_ref, cw_ref):
        ext_ref[idx, CONV_HALO:CONV_HALO + blk, :] = x_ref[...].astype(F32)
        y = cw_ref[CONV_WIDTH - 1:CONV_WIDTH, :] * ext_ref[idx, CONV_HALO:CONV_HALO + blk, :]
        for j in range(CONV_WIDTH - 1):
            off = CONV_HALO - (CONV_WIDTH - 1) + j
            y = y + cw_ref[j:j + 1, :] * ext_ref[idx, off:off + blk, :]
        ext_ref[idx, 0:CONV_HALO, :] = ext_ref[idx, blk:blk + CONV_HALO, :]
        return _silu(y)

    q_all = conv_silu(0, q_ref, cwq_ref)
    k_all = conv_silu(1, k_ref, cwk_ref)
    v_all = conv_silu(2, v_ref, cwv_ref)

    ba = ba_ref[...]
    lane = lax.broadcasted_iota(jnp.int32, ba.shape, 1)
    beta_all = 1.0 / (1.0 + jnp.exp(-ba))
    g_all = -jnp.exp(alog_ref[...]) * _softplus(ba + dtb_ref[...])

    row = lax.broadcasted_iota(jnp.int32, (PAIR, PAIR), 0)
    col = lax.broadcasted_iota(jnp.int32, (PAIR, PAIR), 1)
    col2 = lax.broadcasted_iota(jnp.int32, (2 * PAIR, PAIR), 1)
    same = (row & -CHUNK) == (col & -CHUNK)
    incl = same & (row >= col)
    strict = same & (row > col)
    cum_mat = jnp.where(incl, 1.0, 0.0).astype(BF16)

    probs = []
    qs, ks, vs, betas, gs = [], [], [], [], []
    for hh in range(hb):
        head = hg * hb + hh
        c0 = hh * GDN_D
        qh, kh, vh = q_all[:, c0:c0 + GDN_D], k_all[:, c0:c0 + GDN_D], v_all[:, c0:c0 + GDN_D]
        qh = qh * (lax.rsqrt(jnp.sum(qh * qh, axis=-1, keepdims=True) + EPS) * (GDN_D ** -0.5))
        kh = kh * lax.rsqrt(jnp.sum(kh * kh, axis=-1, keepdims=True) + EPS)
        beta = jnp.sum(jnp.where(lane == head, beta_all, 0.0), axis=-1, keepdims=True)
        g = jnp.sum(jnp.where(lane == head + GDN_HEADS, g_all, 0.0), axis=-1, keepdims=True)
        for p in range(npair):
            r0 = p * PAIR
            probs.append((hh, r0))
            qs.append(qh[r0:r0 + PAIR])
            ks.append(kh[r0:r0 + PAIR])
            vs.append(vh[r0:r0 + PAIR])
            betas.append(beta[r0:r0 + PAIR])
            gs.append(g[r0:r0 + PAIR])
    n = len(probs)
    idx = range(n)

    g_split = [_split_bf16(jnp.broadcast_to(gs[i], (PAIR, PAIR))) for i in idx]
    gcs = [_dot(cum_mat, g_split[i][0]) + _dot(cum_mat, g_split[i][1]) for i in idx]
    decays = [jnp.exp(jnp.where(incl, gcs[i] - gcs[i].T, -jnp.inf)) for i in idx]
    kbs = [ks[i] * betas[i] for i in idx]
    kbf = [ks[i].astype(BF16) for i in idx]
    a_lows = [jnp.where(strict, _dot_nt(kbs[i].astype(BF16), kbf[i]) * decays[i], 0.0) for i in idx]
    attns = [(_dot_nt(qs[i].astype(BF16), kbf[i]) * decays[i]).astype(BF16) for i in idx]

    eye = jnp.where(row == col, 1.0, 0.0)
    ts = [eye - jnp.where((row - col == 1) & ((row & 1) == 1), a_lows[i], 0.0) for i in idx]
    s = 2
    while s < CHUNK:
        sub = ((row & -(2 * s)) == (col & -(2 * s))) & ((row & s) != 0) & ((col & s) == 0)
        tbs = [ts[i].astype(BF16) for i in idx]
        xs = [_dot(jnp.where(sub, a_lows[i], 0.0).astype(BF16), tbs[i]).astype(BF16) for i in idx]
        ts = [ts[i] - _dot(tbs[i], xs[i]) for i in idx]
        s *= 2

    egs = [jnp.exp(gcs[i]) for i in idx]
    rhs = [jnp.concatenate([vs[i] * betas[i], kbs[i] * egs[i]], axis=-1).astype(BF16) for i in idx]
    sols = [_dot(ts[i].astype(BF16), rhs[i]) for i in idx]
    aws = [_dot(attns[i], sols[i].astype(BF16)) for i in idx]
    qps = [(qs[i] * egs[i] - aws[i][:, GDN_D:]).astype(BF16) for i in idx]
    g_last = [[gcs[i][c * CHUNK + CHUNK - 1:(c + 1) * CHUNK, :] for c in range(2)] for i in idx]
    kds = [(ks[i] * jnp.exp(jnp.where(row < CHUNK, g_last[i][0], g_last[i][1]) - gcs[i])).astype(BF16)
           for i in idx]
    sol_ts = [sols[i].T for i in idx]
    pns = [[_dot(jnp.where((col2 & CHUNK) == c * CHUNK, sol_ts[i], 0.0).astype(BF16), kds[i])
            for c in range(2)] for i in idx]

    states = [state_ref[hh] for hh in range(hb)]
    inters = [[None, None] for _ in idx]
    for p in range(npair):
        for c in range(2):
            for hh in range(hb):
                i = hh * npair + p
                st = states[hh]
                st_bf = st.astype(BF16)
                inters[i][c] = _dot_nt(qps[i][c * CHUNK:(c + 1) * CHUNK], st_bf)
                pn = pns[i][c]
                states[hh] = (st * jnp.exp(g_last[i][c]) - _dot(st_bf, pn[GDN_D:].astype(BF16))
                              + pn[:GDN_D])
    for hh in range(hb):
        state_ref[hh] = states[hh]

    og = og_ref[...]
    for i in idx:
        hh, r0 = probs[i]
        c0 = hh * GDN_D
        o = jnp.concatenate(inters[i], axis=0) + aws[i][:, :GDN_D]
        zg = z_ref[r0:r0 + PAIR, c0:c0 + GDN_D].astype(F32)
        o = o * _rms_scale(o) * og * _silu(zg)
        o_ref[r0:r0 + PAIR, c0:c0 + GDN_D] = o.astype(o_ref.dtype)


def _gdn_mixer(proj3, ba3, conv_w8, alog_pad, dtb_pad, o_gain, *, blk, hb):
    b, s, _ = proj3.shape
    ng = GDN_HEADS // hb
    width = hb * GDN_D
    tok = lambda off: pl.BlockSpec((None, blk, width), lambda bi, gi, li: (bi, li, gi + off))
    cw = lambda off: pl.BlockSpec((8, width), lambda bi, gi, li: (0, gi + off))
    vec = pl.BlockSpec((1, LANES), lambda bi, gi, li: (0, 0))
    return pl.pallas_call(
        functools.partial(_gdn_kernel, blk=blk, hb=hb),
        out_shape=jax.ShapeDtypeStruct((b, s, GDN_HEADS * GDN_D), BF16),
        grid=(b, ng, s // blk),
        in_specs=[tok(0), tok(ng), tok(2 * ng), tok(3 * ng),
                  pl.BlockSpec((None, blk, LANES), lambda bi, gi, li: (bi, li, 0)),
                  cw(0), cw(ng), cw(2 * ng), vec, vec, vec],
        out_specs=pl.BlockSpec((None, blk, width), lambda bi, gi, li: (bi, li, gi)),
        scratch_shapes=[pltpu.VMEM((hb, GDN_D, GDN_D), F32),
                        pltpu.VMEM((3, blk + CONV_HALO, width), F32)],
        compiler_params=pltpu.CompilerParams(
            dimension_semantics=("parallel", "parallel", "arbitrary"), vmem_limit_bytes=VMEM_LIMIT),
        name="gdn_mixer",
    )(proj3, proj3, proj3, proj3, ba3, conv_w8, conv_w8, conv_w8, alog_pad, dtb_pad, o_gain)


def _proj_ffn_kernel(a_ref, wo_ref, res_ref, g_ref, wg_ref, wu_ref, wd_ref, o_ref,
                     h1_ref, xn_ref, acc_ref):
    j = pl.program_id(1)

    @pl.when(j == 0)
    def _():
        h1 = res_ref[...] + _dot(a_ref[...], wo_ref[...])
        h1_ref[...] = h1
        xn_ref[...] = (h1 * _rms_scale(h1) * g_ref[...]).astype(BF16)
        acc_ref[...] = jnp.zeros_like(acc_ref)

    xn = xn_ref[...]
    gate = _dot(xn, wg_ref[...])
    up = _dot(xn, wu_ref[...])
    acc_ref[...] += _dot((_silu(gate) * up).astype(BF16), wd_ref[...])

    @pl.when(j == pl.num_programs(1) - 1)
    def _():
        o_ref[...] = h1_ref[...] + acc_ref[...]


def _proj_ffn(a2, w_out, res2, gain, w_gu, w_down, *, tm, tf):
    t, d = res2.shape
    ka = a2.shape[1]
    f = w_down.shape[0]
    nf = f // tf
    return pl.pallas_call(
        _proj_ffn_kernel,
        out_shape=jax.ShapeDtypeStruct((t, d), F32),
        grid=(t // tm, nf),
        in_specs=[pl.BlockSpec((tm, ka), lambda i, j: (i, 0)),
                  pl.BlockSpec((ka, d), lambda i, j: (0, 0)),
                  pl.BlockSpec((tm, d), lambda i, j: (i, 0)),
                  pl.BlockSpec((1, d), lambda i, j: (0, 0)),
                  pl.BlockSpec((d, tf), lambda i, j: (0, j)),
                  pl.BlockSpec((d, tf), lambda i, j: (0, j + nf)),
                  pl.BlockSpec((tf, d), lambda i, j: (j, 0))],
        out_specs=pl.BlockSpec((tm, d), lambda i, j: (i, 0)),
        scratch_shapes=[pltpu.VMEM((tm, d), F32), pltpu.VMEM((tm, d), BF16), pltpu.VMEM((tm, d), F32)],
        compiler_params=pltpu.CompilerParams(
            dimension_semantics=("parallel", "arbitrary"), vmem_limit_bytes=VMEM_LIMIT),
        name="proj_ffn",
    )(a2, w_out, res2, gain, w_gu, w_gu, w_down)


def _sb_qkv_kernel(h_ref, gq_ref, gkv_ref, w_ref, qg_ref, kg_ref, o_ref, xq_ref, xkv_ref, *, nq, nk):
    j = pl.program_id(1)

    @pl.when(j == 0)
    def _():
        x = h_ref[...]
        xs = x * _rms_scale(x)
        xq_ref[...] = (xs * gq_ref[...]).astype(BF16)
        xkv_ref[...] = (xs * gkv_ref[...]).astype(BF16)

    def head_norm(y, gain):
        for c in range(y.shape[1] // SB_DH):
            yc = y[:, c * SB_DH:(c + 1) * SB_DH]
            o_ref[:, c * SB_DH:(c + 1) * SB_DH] = (yc * _rms_scale(yc) * gain).astype(o_ref.dtype)

    @pl.when(j < nq)
    def _():
        head_norm(_dot(xq_ref[...], w_ref[...]), qg_ref[...] * (SB_DH ** -0.5 * LOG2E))

    @pl.when((j >= nq) & (j < nq + nk))
    def _():
        head_norm(_dot(xkv_ref[...], w_ref[...]), kg_ref[...])

    @pl.when(j >= nq + nk)
    def _():
        o_ref[...] = _dot(xkv_ref[...], w_ref[...]).astype(o_ref.dtype)


def _sb_qkv(h2, gain_q, gain_kv, w_qkv, q_gain, k_gain, *, tm, tn):
    t, d = h2.shape
    n = w_qkv.shape[1]
    nq = (SB_HEADS * SB_DH) // tn
    vec = lambda width: pl.BlockSpec((1, width), lambda i, j: (0, 0))
    return pl.pallas_call(
        functools.partial(_sb_qkv_kernel, nq=nq, nk=nq),
        out_shape=jax.ShapeDtypeStruct((t, n), BF16),
        grid=(t // tm, n // tn),
        in_specs=[pl.BlockSpec((tm, d), lambda i, j: (i, 0)), vec(d), vec(d),
                  pl.BlockSpec((d, tn), lambda i, j: (0, j)), vec(SB_DH), vec(SB_DH)],
        out_specs=pl.BlockSpec((tm, tn), lambda i, j: (i, j)),
        scratch_shapes=[pltpu.VMEM((tm, d), BF16), pltpu.VMEM((tm, d), BF16)],
        compiler_params=pltpu.CompilerParams(
            dimension_semantics=("parallel", "arbitrary"), vmem_limit_bytes=VMEM_LIMIT),
        name="sb_qkv",
    )(h2, gain_q, gain_kv, w_qkv, q_gain, k_gain)


def _sb_attn_kernel(q_ref, k_ref, v_ref, suf_ref, o_ref, *, tk):
    tq = 2 * tk
    qi = pl.program_id(2)
    q = q_ref[...]
    suf = suf_ref[...]
    ahead = (lax.broadcasted_iota(jnp.int32, (tq, tk), 0)
             - lax.broadcasted_iota(jnp.int32, (tq, tk), 1))

    def scores(j):
        z = _dot_nt(q, k_ref[pl.ds(pl.multiple_of(j * tk, tk), tk), :])
        return z, jnp.maximum(z, 0.0) + jnp.log2(1.0 + jnp.exp2(-jnp.abs(z)))

    def absorb(j, z, sp, rest, acc, causal):
        fail = sp if causal is None else jnp.where(causal, sp, 0.0)
        after = _dot(fail.astype(BF16), suf) + rest
        att = jnp.exp2((z - sp) - after)
        if causal is not None:
            att = jnp.where(causal, att, 0.0)
        acc = acc + _dot(att.astype(BF16), v_ref[pl.ds(pl.multiple_of(j * tk, tk), tk), :])
        return rest + jnp.sum(fail, axis=-1, keepdims=True), acc

    def two_blocks(j, rest, acc, diagonal):
        z1, sp1 = scores(j + 1)
        z0, sp0 = scores(j)
        rest, acc = absorb(j + 1, z1, sp1, rest, acc, ahead > tk if diagonal else None)
        return absorb(j, z0, sp0, rest, acc, ahead > 0 if diagonal else None)

    rest, acc = two_blocks(2 * qi, jnp.zeros((tq, 1), F32), jnp.zeros((tq, SB_DH), F32), True)

    def body(it, carry):
        return two_blocks(2 * (qi - 1 - it), carry[0], carry[1], False)

    rest, acc = lax.fori_loop(0, qi, body, (rest, acc))
    o_ref[...] = acc.astype(o_ref.dtype)


def _sb_attn(qkv3, suffix_mat, *, tk):
    b, s, _ = qkv3.shape
    h = SB_HEADS
    tq = 2 * tk
    return pl.pallas_call(
        functools.partial(_sb_attn_kernel, tk=tk),
        out_shape=jax.ShapeDtypeStruct((b, s, h * SB_DH), BF16),
        grid=(b, h, s // tq),
        in_specs=[pl.BlockSpec((None, tq, SB_DH), lambda bi, hi, qi: (bi, qi, hi)),
                  pl.BlockSpec((None, s, SB_DH), lambda bi, hi, qi: (bi, 0, hi + h)),
                  pl.BlockSpec((None, s, SB_DH), lambda bi, hi, qi: (bi, 0, hi + 2 * h)),
                  pl.BlockSpec((tk, tk), lambda bi, hi, qi: (0, 0))],
        out_specs=pl.BlockSpec((None, tq, SB_DH), lambda bi, hi, qi: (bi, qi, hi)),
        compiler_params=pltpu.CompilerParams(
            dimension_semantics=("parallel", "parallel", "arbitrary"), vmem_limit_bytes=VMEM_LIMIT),
        name="sb_attn",
    )(qkv3, qkv3, qkv3, suffix_mat)


def _pick(n, prefs):
    for p in prefs:
        if n % p == 0:
            return p
    return n


def kernel(x, attn_norm, ffn_norm, gdn_w_in, gdn_conv_w, gdn_a_log, gdn_dt_bias, gdn_o_gain, gdn_w_out,
           kv_norm, w_kv, k_gain, sb_w_q, sb_q_gain, sb_w_out, ffn_w_gu, ffn_w_down):
    b, s, d = x.shape
    t = b * s
    hd = GDN_HEADS * GDN_D
    n_main = 4 * hd
    tm = _pick(t, (1024, 512, 256, 128))
    gdn_blk = _pick(s, (512, 256, 128))
    tk = _pick(s // 2, (256, 128))

    row = lambda vec: vec.reshape(1, -1).astype(F32)
    x2 = x.reshape(t, d)

    w_in = gdn_w_in[0]
    w_ba = jnp.pad(w_in[:, n_main:], ((0, 0), (0, LANES - 2 * GDN_HEADS)))
    wba_hi = w_ba.astype(BF16)
    wba_lo = (w_ba - wba_hi.astype(F32)).astype(BF16)
    proj, ba = _gdn_inproj(x2, row(attn_norm[0]), w_in[:, :n_main].astype(BF16), wba_hi, wba_lo,
                           tm=tm, tn=1024, out_dtype=F32)
    lane_pad = lambda vec: jnp.pad(vec.astype(F32), (GDN_HEADS, LANES - 2 * GDN_HEADS)).reshape(1, LANES)
    conv_w8 = jnp.pad(gdn_conv_w[0].astype(F32), ((0, 8 - CONV_WIDTH), (0, 0)))
    o_gdn = _gdn_mixer(proj.reshape(b, s, n_main), ba.reshape(b, s, LANES), conv_w8,
                       lane_pad(gdn_a_log[0]), lane_pad(gdn_dt_bias[0]), row(gdn_o_gain[0]),
                       blk=gdn_blk, hb=2)
    h2 = _proj_ffn(o_gdn.reshape(t, hd), gdn_w_out[0].astype(BF16), x2, row(ffn_norm[0]),
                   ffn_w_gu[0].astype(BF16), ffn_w_down[0].astype(BF16), tm=tm, tf=256)

    w_qkv = jnp.concatenate([sb_w_q[0], w_kv], axis=1).astype(BF16)
    qkv = _sb_qkv(h2, row(attn_norm[1]), row(kv_norm), w_qkv, row(sb_q_gain[0]), row(k_gain),
                  tm=tm, tn=1024)
    idx = jnp.arange(tk)
    suffix_mat = (idx[:, None] > idx[None, :]).astype(BF16)
    o_sb = _sb_attn(qkv.reshape(b, s, 3 * SB_HEADS * SB_DH), suffix_mat, tk=tk)
    h2 = _proj_ffn(o_sb.reshape(t, SB_HEADS * SB_DH), sb_w_out[0].astype(BF16), h2, row(ffn_norm[1]),
                   ffn_w_gu[1].astype(BF16), ffn_w_down[1].astype(BF16), tm=tm, tf=256)
    return h2.reshape(b, s, d)
```

```python
import functools

import jax
import jax.numpy as jnp
from jax import lax
from jax.experimental import pallas as pl
from jax.experimental.pallas import tpu as pltpu

F32 = jnp.float32
BF16 = jnp.bfloat16
EPS = 1e-6
LOG2E = 1.4426950408889634

LANES = 128
GDN_HEADS = 8
GDN_D = 128
CHUNK = 64
PAIR = 2 * CHUNK
CONV_WIDTH = 4
CONV_HALO = 8
SB_HEADS = 8
SB_DH = 128
VMEM_LIMIT = 56 * 1024 * 1024
ROW_CHUNK = 512
COL_CHUNK = 1024
FF_CHUNK = 256
DEAD_LOG2 = 160.0


def _dot(a, b):
    return jnp.dot(a, b, preferred_element_type=F32)


def _dot_nt(a, b):
    return lax.dot_general(a, b, (((1,), (1,)), ((), ())), preferred_element_type=F32)


def _silu(x):
    return x * (1.0 / (1.0 + jnp.exp(-x)))


def _softplus(x):
    return jnp.maximum(x, 0.0) + jnp.log(1.0 + jnp.exp(-jnp.abs(x)))


def _split_bf16(x):
    hi = x.astype(BF16)
    lo = (x - hi.astype(F32)).astype(BF16)
    return hi, lo


def _rms_scale(x):
    return lax.rsqrt(jnp.mean(x * x, axis=-1, keepdims=True) + EPS)


def _resident(shape):
    return pl.BlockSpec(shape, lambda *_: (0,) * len(shape), pipeline_mode=pl.Buffered(1))


def _inproj_kernel(x_ref, g_ref, w_ref, wba_hi_ref, wba_lo_ref, y_ref, ba_ref):
    tm, n = y_ref.shape
    for r0 in range(0, tm, ROW_CHUNK):
        rows = slice(r0, r0 + ROW_CHUNK)
        x = x_ref[rows, :]
        xh, xl = _split_bf16(x * _rms_scale(x) * g_ref[...])
        ba_ref[rows, :] = (_dot(xh, wba_hi_ref[...]) + _dot(xl, wba_hi_ref[...])
                           + _dot(xh, wba_lo_ref[...]))
        for c0 in range(0, n, COL_CHUNK):
            cols = slice(c0, c0 + COL_CHUNK)
            y_ref[rows, cols] = _dot(xh, w_ref[:, cols]).astype(y_ref.dtype)


def _gdn_inproj(x2, gain, w_main, wba_hi, wba_lo, *, tm):
    t, d = x2.shape
    n = w_main.shape[1]
    return pl.pallas_call(
        _inproj_kernel,
        out_shape=(jax.ShapeDtypeStruct((t, n), BF16),
                   jax.ShapeDtypeStruct((t, LANES), F32)),
        grid=(t // tm,),
        in_specs=[pl.BlockSpec((tm, d), lambda i: (i, 0)),
                  _resident((1, d)), _resident((d, n)), _resident((d, LANES)), _resident((d, LANES))],
        out_specs=(pl.BlockSpec((tm, n), lambda i: (i, 0)),
                   pl.BlockSpec((tm, LANES), lambda i: (i, 0))),
        compiler_params=pltpu.CompilerParams(
            dimension_semantics=("parallel",), vmem_limit_bytes=VMEM_LIMIT),
        name="gdn_inproj",
    )(x2, gain, w_main, wba_hi, wba_lo)


def _gdn_kernel(q_ref, k_ref, v_ref, z_ref, ba_ref, cwq_ref, cwk_ref, cwv_ref,
                alog_ref, dtb_ref, og_ref, cum_ref, o_ref, state_ref, ext_ref, *, blk, hb):
    hg = pl.program_id(1)
    width = hb * GDN_D
    npair = blk // PAIR

    @pl.when(pl.program_id(2) == 0)
    def _():
        state_ref[...] = jnp.zeros_like(state_ref)
        ext_ref[:, 0:CONV_HALO, :] = jnp.zeros((3, CONV_HALO, width), F32)

    def conv_silu(idx, x_ref, cw_ref):
        ext_ref[idx, CONV_HALO:CONV_HALO + blk, :] = x_ref[...].astype(F32)
        y = cw_ref[CONV_WIDTH - 1:CONV_WIDTH, :] * ext_ref[idx, CONV_HALO:CONV_HALO + blk, :]
        for j in range(CONV_WIDTH - 1):
            off = CONV_HALO - (CONV_WIDTH - 1) + j
            y = y + cw_ref[j:j + 1, :] * ext_ref[idx, off:off + blk, :]
        ext_ref[idx, 0:CONV_HALO, :] = ext_ref[idx, blk:blk + CONV_HALO, :]
        return _silu(y)

    q_all = conv_silu(0, q_ref, cwq_ref)
    k_all = conv_silu(1, k_ref, cwk_ref)
    v_all = conv_silu(2, v_ref, cwv_ref)

    ba = ba_ref[...]
    lane = lax.broadcasted_iota(jnp.int32, ba.shape, 1)
    beta_all = 1.0 / (1.0 + jnp.exp(-ba))
    g_hi, g_lo = _split_bf16(-jnp.exp(alog_ref[...]) * _softplus(ba + dtb_ref[...]))
    gc_all = _dot(cum_ref[...], g_hi) + _dot(cum_ref[...], g_lo)

    row = lax.broadcasted_iota(jnp.int32, (PAIR, PAIR), 0)
    col = lax.broadcasted_iota(jnp.int32, (PAIR, PAIR), 1)
    col2 = lax.broadcasted_iota(jnp.int32, (2 * PAIR, PAIR), 1)
    same = (row & -CHUNK) == (col & -CHUNK)
    incl = same & (row >= col)
    strict = same & (row > col)

    probs = []
    qs, ks, vs, betas, gcs = [], [], [], [], []
    for hh in range(hb):
        head = hg * hb + hh
        c0 = hh * GDN_D
        qh, kh, vh = q_all[:, c0:c0 + GDN_D], k_all[:, c0:c0 + GDN_D], v_all[:, c0:c0 + GDN_D]
        qh = qh * (lax.rsqrt(jnp.sum(qh * qh, axis=-1, keepdims=True) + EPS) * (GDN_D ** -0.5))
        kh = kh * lax.rsqrt(jnp.sum(kh * kh, axis=-1, keepdims=True) + EPS)
        beta = jnp.sum(jnp.where(lane == head, beta_all, 0.0), axis=-1, keepdims=True)
        gc = jnp.sum(jnp.where(lane == head + GDN_HEADS, gc_all, 0.0), axis=-1, keepdims=True)
        for p in range(npair):
            r0 = p * PAIR
            probs.append((hh, r0))
            qs.append(qh[r0:r0 + PAIR])
            ks.append(kh[r0:r0 + PAIR])
            vs.append(vh[r0:r0 + PAIR])
            betas.append(beta[r0:r0 + PAIR])
            gcs.append(jnp.broadcast_to(gc[r0:r0 + PAIR], (PAIR, PAIR)))
    n = len(probs)
    idx = range(n)

    decays = [jnp.exp(jnp.where(incl, gcs[i] - gcs[i].T, -jnp.inf)) for i in idx]
    kbs = [ks[i] * betas[i] for i in idx]
    kbf = [ks[i].astype(BF16) for i in idx]
    a_lows = [jnp.where(strict, _dot_nt(kbs[i].astype(BF16), kbf[i]) * decays[i], 0.0) for i in idx]
    attns = [(_dot_nt(qs[i].astype(BF16), kbf[i]) * decays[i]).astype(BF16) for i in idx]

    eye = jnp.where(row == col, 1.0, 0.0)
    ts = [eye - jnp.where((row - col == 1) & ((row & 1) == 1), a_lows[i], 0.0) for i in idx]
    s = 2
    while s < CHUNK:
        sub = ((row & -(2 * s)) == (col & -(2 * s))) & ((row & s) != 0) & ((col & s) == 0)
        tbs = [ts[i].astype(BF16) for i in idx]
        xs = [_dot(jnp.where(sub, a_lows[i], 0.0).astype(BF16), tbs[i]).astype(BF16) for i in idx]
        ts = [ts[i] - _dot(tbs[i], xs[i]) for i in idx]
        s *= 2

    egs = [jnp.exp(gcs[i]) for i in idx]
    rhs = [jnp.concatenate([vs[i] * betas[i], kbs[i] * egs[i]], axis=-1).astype(BF16) for i in idx]
    sols = [_dot(ts[i].astype(BF16), rhs[i]) for i in idx]
    aws = [_dot(attns[i], sols[i].astype(BF16)) for i in idx]
    qps = [(qs[i] * egs[i] - aws[i][:, GDN_D:]).astype(BF16) for i in idx]
    g_last = [[gcs[i][c * CHUNK + CHUNK - 1:(c + 1) * CHUNK, :] for c in range(2)] for i in idx]
    kds = [(ks[i] * jnp.exp(jnp.where(row < CHUNK, g_last[i][0], g_last[i][1]) - gcs[i])).astype(BF16)
           for i in idx]
    sol_ts = [sols[i].T for i in idx]
    pns = [[_dot(jnp.where((col2 & CHUNK) == c * CHUNK, sol_ts[i], 0.0).astype(BF16), kds[i])
            for c in range(2)] for i in idx]

    states = [state_ref[hh] for hh in range(hb)]
    inters = [[None, None] for _ in idx]
    for p in range(npair):
        for c in range(2):
            for hh in range(hb):
                i = hh * npair + p
                st = states[hh]
                st_bf = st.astype(BF16)
                inters[i][c] = _dot_nt(qps[i][c * CHUNK:(c + 1) * CHUNK], st_bf)
                pn = pns[i][c]
                states[hh] = (st * jnp.exp(g_last[i][c]) - _dot(st_bf, pn[GDN_D:].astype(BF16))
                              + pn[:GDN_D])
    for hh in range(hb):
        state_ref[hh] = states[hh]

    og = og_ref[...]
    for i in idx:
        hh, r0 = probs[i]
        c0 = hh * GDN_D
        o = jnp.concatenate(inters[i], axis=0) + aws[i][:, :GDN_D]
        zg = z_ref[r0:r0 + PAIR, c0:c0 + GDN_D].astype(F32)
        o = o * _rms_scale(o) * og * _silu(zg)
        o_ref[r0:r0 + PAIR, c0:c0 + GDN_D] = o.astype(o_ref.dtype)


def _gdn_mixer(proj3, ba3, conv_w8, alog_pad, dtb_pad, o_gain, cum_mat, *, blk, hb):
    b, s, _ = proj3.shape
    ng = GDN_HEADS // hb
    width = hb * GDN_D
    tok = lambda off: pl.BlockSpec((None, blk, width), lambda bi, gi, li: (bi, li, gi + off))
    cw = lambda off: pl.BlockSpec((8, width), lambda bi, gi, li: (0, gi + off))
    vec = pl.BlockSpec((1, LANES), lambda bi, gi, li: (0, 0))
    return pl.pallas_call(
        functools.partial(_gdn_kernel, blk=blk, hb=hb),
        out_shape=jax.ShapeDtypeStruct((b, s, GDN_HEADS * GDN_D), BF16),
        grid=(b, ng, s // blk),
        in_specs=[tok(0), tok(ng), tok(2 * ng), tok(3 * ng),
                  pl.BlockSpec((None, blk, LANES), lambda bi, gi, li: (bi, li, 0)),
                  cw(0), cw(ng), cw(2 * ng), vec, vec, vec,
                  pl.BlockSpec((blk, blk), lambda bi, gi, li: (0, 0))],
        out_specs=pl.BlockSpec((None, blk, width), lambda bi, gi, li: (bi, li, gi)),
        scratch_shapes=[pltpu.VMEM((hb, GDN_D, GDN_D), F32),
                        pltpu.VMEM((3, blk + CONV_HALO, width), F32)],
        compiler_params=pltpu.CompilerParams(
            dimension_semantics=("parallel", "parallel", "arbitrary"), vmem_limit_bytes=VMEM_LIMIT),
        name="gdn_mixer",
    )(proj3, proj3, proj3, proj3, ba3, conv_w8, conv_w8, conv_w8, alog_pad, dtb_pad, o_gain, cum_mat)


def _proj_ffn_kernel(a_ref, wo_ref, res_ref, g_ref, wgu_ref, wd_ref, o_ref, mid_ref):
    f = wd_ref.shape[0]
    h1 = res_ref[...] + _dot(a_ref[...], wo_ref[...])
    o_ref[...] = h1
    xn = (h1 * _rms_scale(h1) * g_ref[...]).astype(BF16)
    for c0 in range(0, f, FF_CHUNK):
        gate = _dot(xn, wgu_ref[:, c0:c0 + FF_CHUNK])
        up = _dot(xn, wgu_ref[:, f + c0:f + c0 + FF_CHUNK])
        mid_ref[:, c0:c0 + FF_CHUNK] = (_silu(gate) * up).astype(BF16)
    o_ref[...] += _dot(mid_ref[...], wd_ref[...])


def _proj_ffn(a2, w_out, res2, gain, w_gu, w_down, *, tm):
    t, d = res2.shape
    ka = a2.shape[1]
    f = w_down.shape[0]
    return pl.pallas_call(
        _proj_ffn_kernel,
        out_shape=jax.ShapeDtypeStruct((t, d), F32),
        grid=(t // tm,),
        in_specs=[pl.BlockSpec((tm, ka), lambda i: (i, 0)),
                  _resident((ka, d)),
                  pl.BlockSpec((tm, d), lambda i: (i, 0)),
                  _resident((1, d)), _resident((d, 2 * f)), _resident((f, d))],
        out_specs=pl.BlockSpec((tm, d), lambda i: (i, 0)),
        scratch_shapes=[pltpu.VMEM((tm, f), BF16)],
        compiler_params=pltpu.CompilerParams(
            dimension_semantics=("parallel",), vmem_limit_bytes=VMEM_LIMIT),
        name="proj_ffn",
    )(a2, w_out, res2, gain, w_gu, w_down)


def _sb_qkv_kernel(h_ref, gq_ref, gkv_ref, w_ref, qg_ref, kg_ref, o_ref):
    tm = h_ref.shape[0]
    nq = SB_HEADS * SB_DH

    def head_norm(rows, c0, y, gain):
        for c in range(0, y.shape[1], SB_DH):
            yc = y[:, c:c + SB_DH]
            o_ref[rows, c0 + c:c0 + c + SB_DH] = (yc * _rms_scale(yc) * gain).astype(o_ref.dtype)

    for r0 in range(0, tm, ROW_CHUNK):
        rows = slice(r0, r0 + ROW_CHUNK)
        x = h_ref[rows, :]
        xs = x * _rms_scale(x)
        xq = (xs * gq_ref[...]).astype(BF16)
        xkv = (xs * gkv_ref[...]).astype(BF16)
        head_norm(rows, 0, _dot(xq, w_ref[:, 0:nq]), qg_ref[...] * (SB_DH ** -0.5 * LOG2E))
        head_norm(rows, nq, _dot(xkv, w_ref[:, nq:2 * nq]), kg_ref[...])
        o_ref[rows, 2 * nq:3 * nq] = _dot(xkv, w_ref[:, 2 * nq:3 * nq]).astype(o_ref.dtype)


def _sb_qkv(h2, gain_q, gain_kv, w_qkv, q_gain, k_gain, *, tm):
    t, d = h2.shape
    n = w_qkv.shape[1]
    return pl.pallas_call(
        _sb_qkv_kernel,
        out_shape=jax.ShapeDtypeStruct((t, n), BF16),
        grid=(t // tm,),
        in_specs=[pl.BlockSpec((tm, d), lambda i: (i, 0)), _resident((1, d)), _resident((1, d)),
                  _resident((d, n)), _resident((1, SB_DH)), _resident((1, SB_DH))],
        out_specs=pl.BlockSpec((tm, n), lambda i: (i, 0)),
        compiler_params=pltpu.CompilerParams(
            dimension_semantics=("parallel",), vmem_limit_bytes=VMEM_LIMIT),
        name="sb_qkv",
    )(h2, gain_q, gain_kv, w_qkv, q_gain, k_gain)


def _sb_attn_kernel(q_ref, k_ref, v_ref, suf_ref, o_ref, *, tk, hb):
    tq = 2 * tk
    qi = pl.program_id(2)
    suf = suf_ref[...]
    ahead = (lax.broadcasted_iota(jnp.int32, (tq, tk), 0)
             - lax.broadcasted_iota(jnp.int32, (tq, tk), 1))
    heads = [slice(hh * SB_DH, (hh + 1) * SB_DH) for hh in range(hb)]
    qs = [q_ref[:, hs] for hs in heads]

    def scores(hh, j):
        z = _dot_nt(qs[hh], k_ref[pl.ds(pl.multiple_of(j * tk, tk), tk), heads[hh]])
        return z, jnp.maximum(z, 0.0) + jnp.log2(1.0 + jnp.exp2(-jnp.abs(z)))

    def absorb(hh, j, z, sp, rest, acc, causal):
        fail = sp if causal is None else jnp.where(causal, sp, 0.0)
        after = _dot(fail.astype(BF16), suf) + rest
        att = jnp.exp2((z - sp) - after)
        if causal is not None:
            att = jnp.where(causal, att, 0.0)
        vb = v_ref[pl.ds(pl.multiple_of(j * tk, tk), tk), heads[hh]]
        return rest + jnp.sum(fail, axis=-1, keepdims=True), acc + _dot(att.astype(BF16), vb)

    zs = [[scores(hh, 2 * qi + 1 - d) for d in range(2)] for hh in range(hb)]
    carry = []
    for hh in range(hb):
        rest, acc = jnp.zeros((tq, 1), F32), jnp.zeros((tq, SB_DH), F32)
        rest, acc = absorb(hh, 2 * qi + 1, *zs[hh][0], rest, acc, ahead > tk)
        rest, acc = absorb(hh, 2 * qi, *zs[hh][1], rest, acc, ahead > 0)
        carry += [rest, acc]

    def low_water(carry):
        return functools.reduce(jnp.minimum, [jnp.min(carry[2 * hh]) for hh in range(hb)])

    def cond(state):
        it, low = state[0], state[1]
        return (it < 2 * qi) & (low < DEAD_LOG2)

    def body(state):
        it, carry = state[0], list(state[2:])
        j = 2 * qi - 1 - it
        zs = [scores(hh, j) for hh in range(hb)]
        for hh in range(hb):
            carry[2 * hh], carry[2 * hh + 1] = absorb(hh, j, *zs[hh], carry[2 * hh], carry[2 * hh + 1], None)
        return (it + 1, low_water(carry), *carry)

    final = lax.while_loop(cond, body, (jnp.int32(0), low_water(carry), *carry))
    for hh in range(hb):
        o_ref[:, heads[hh]] = final[3 + 2 * hh].astype(o_ref.dtype)


def _sb_attn(qkv3, suffix_mat, *, tk, hb):
    b, s, _ = qkv3.shape
    ng = SB_HEADS // hb
    width = hb * SB_DH
    tq = 2 * tk
    return pl.pallas_call(
        functools.partial(_sb_attn_kernel, tk=tk, hb=hb),
        out_shape=jax.ShapeDtypeStruct((b, s, SB_HEADS * SB_DH), BF16),
        grid=(b, ng, s // tq),
        in_specs=[pl.BlockSpec((None, tq, width), lambda bi, gi, qi: (bi, qi, gi)),
                  pl.BlockSpec((None, s, width), lambda bi, gi, qi: (bi, 0, gi + ng)),
                  pl.BlockSpec((None, s, width), lambda bi, gi, qi: (bi, 0, gi + 2 * ng)),
                  pl.BlockSpec((tk, tk), lambda bi, gi, qi: (0, 0))],
        out_specs=pl.BlockSpec((None, tq, width), lambda bi, gi, qi: (bi, qi, gi)),
        compiler_params=pltpu.CompilerParams(
            dimension_semantics=("parallel", "parallel", "arbitrary"), vmem_limit_bytes=VMEM_LIMIT),
        name="sb_attn",
    )(qkv3, qkv3, qkv3, suffix_mat)


def _pick(n, prefs):
    for p in prefs:
        if n % p == 0:
            return p
    return n


def kernel(x, attn_norm, ffn_norm, gdn_w_in, gdn_conv_w, gdn_a_log, gdn_dt_bias, gdn_o_gain, gdn_w_out,
           kv_norm, w_kv, k_gain, sb_w_q, sb_q_gain, sb_w_out, ffn_w_gu, ffn_w_down):
    b, s, d = x.shape
    t = b * s
    hd = GDN_HEADS * GDN_D
    n_main = 4 * hd
    tm = _pick(t, (1024, ROW_CHUNK))
    tm_ffn = ROW_CHUNK
    gdn_blk = _pick(s, (512, 256, 128))
    tk = _pick(s // 2, (256, 128))

    row = lambda vec: vec.reshape(1, -1).astype(F32)
    x2 = x.reshape(t, d)

    w_in = gdn_w_in[0]
    w_ba = jnp.pad(w_in[:, n_main:], ((0, 0), (0, LANES - 2 * GDN_HEADS)))
    wba_hi = w_ba.astype(BF16)
    wba_lo = (w_ba - wba_hi.astype(F32)).astype(BF16)
    proj, ba = _gdn_inproj(x2, row(attn_norm[0]), w_in[:, :n_main].astype(BF16), wba_hi, wba_lo, tm=tm)
    lane_pad = lambda vec: jnp.pad(vec.astype(F32), (GDN_HEADS, LANES - 2 * GDN_HEADS)).reshape(1, LANES)
    conv_w8 = jnp.pad(gdn_conv_w[0].astype(F32), ((0, 8 - CONV_WIDTH), (0, 0)))
    tok = jnp.arange(gdn_blk)
    cum_mat = ((tok[:, None] // CHUNK == tok[None, :] // CHUNK) & (tok[:, None] >= tok[None, :])).astype(BF16)
    o_gdn = _gdn_mixer(proj.reshape(b, s, n_main), ba.reshape(b, s, LANES), conv_w8,
                       lane_pad(gdn_a_log[0]), lane_pad(gdn_dt_bias[0]), row(gdn_o_gain[0]), cum_mat,
                       blk=gdn_blk, hb=2)
    h2 = _proj_ffn(o_gdn.reshape(t, hd), gdn_w_out[0].astype(BF16), x2, row(ffn_norm[0]),
                   ffn_w_gu[0].astype(BF16), ffn_w_down[0].astype(BF16), tm=tm_ffn)

    w_qkv = jnp.concatenate([sb_w_q[0], w_kv], axis=1).astype(BF16)
    qkv = _sb_qkv(h2, row(attn_norm[1]), row(kv_norm), w_qkv, row(sb_q_gain[0]), row(k_gain), tm=tm)
    idx = jnp.arange(tk)
    suffix_mat = (idx[:, None] > idx[None, :]).astype(BF16)
    o_sb = _sb_attn(qkv.reshape(b, s, 3 * SB_HEADS * SB_DH), suffix_mat, tk=tk, hb=2)
    h2 = _proj_ffn(o_sb.reshape(t, SB_HEADS * SB_DH), sb_w_out[0].astype(BF16), h2, row(ffn_norm[1]),
                   ffn_w_gu[1].astype(BF16), ffn_w_down[1].astype(BF16), tm=tm_ffn)
    return h2.reshape(b, s, d)
```

```python
import functools

import jax
import jax.numpy as jnp
from jax import lax
from jax.experimental import pallas as pl
from jax.experimental.pallas import tpu as pltpu

F32 = jnp.float32
BF16 = jnp.bfloat16
EPS = 1e-6
LOG2E = 1.4426950408889634

LANES = 128
GDN_HEADS = 8
GDN_D = 128
CHUNK = 64
PAIR = 2 * CHUNK
CONV_WIDTH = 4
CONV_HALO = 8
SB_HEADS = 8
SB_DH = 128
VMEM_LIMIT = 56 * 1024 * 1024
ROW_CHUNK = 512
COL_CHUNK = 1024
FF_CHUNK = 256
DEAD_LOG2 = 160.0


def _dot(a, b):
    return jnp.dot(a, b, preferred_element_type=F32)


def _dot_nt(a, b):
    return lax.dot_general(a, b, (((1,), (1,)), ((), ())), preferred_element_type=F32)


def _silu(x):
    return x * (1.0 / (1.0 + jnp.exp(-x)))


def _softplus(x):
    return jnp.maximum(x, 0.0) + jnp.log(1.0 + jnp.exp(-jnp.abs(x)))


def _split_bf16(x):
    hi = x.astype(BF16)
    lo = (x - hi.astype(F32)).astype(BF16)
    return hi, lo


def _rms_scale(x):
    return lax.rsqrt(jnp.mean(x * x, axis=-1, keepdims=True) + EPS)


def _resident(shape):
    return pl.BlockSpec(shape, lambda *_: (0,) * len(shape), pipeline_mode=pl.Buffered(1))


def _inproj_kernel(x_ref, g_ref, w_ref, wba_hi_ref, wba_lo_ref, y_ref, ba_ref):
    tm, n = y_ref.shape
    for r0 in range(0, tm, ROW_CHUNK):
        rows = slice(r0, r0 + ROW_CHUNK)
        x = x_ref[rows, :]
        xh, xl = _split_bf16(x * _rms_scale(x) * g_ref[...])
        ba_ref[rows, :] = (_dot(xh, wba_hi_ref[...]) + _dot(xl, wba_hi_ref[...])
                           + _dot(xh, wba_lo_ref[...]))
        for c0 in range(0, n, COL_CHUNK):
            cols = slice(c0, c0 + COL_CHUNK)
            y_ref[rows, cols] = _dot(xh, w_ref[:, cols]).astype(y_ref.dtype)


def _gdn_inproj(x2, gain, w_main, wba_hi, wba_lo, *, tm):
    t, d = x2.shape
    n = w_main.shape[1]
    return pl.pallas_call(
        _inproj_kernel,
        out_shape=(jax.ShapeDtypeStruct((t, n), BF16),
                   jax.ShapeDtypeStruct((t, LANES), F32)),
        grid=(t // tm,),
        in_specs=[pl.BlockSpec((tm, d), lambda i: (i, 0)),
                  _resident((1, d)), _resident((d, n)), _resident((d, LANES)), _resident((d, LANES))],
        out_specs=(pl.BlockSpec((tm, n), lambda i: (i, 0)),
                   pl.BlockSpec((tm, LANES), lambda i: (i, 0))),
        compiler_params=pltpu.CompilerParams(
            dimension_semantics=("parallel",), vmem_limit_bytes=VMEM_LIMIT),
        name="gdn_inproj",
    )(x2, gain, w_main, wba_hi, wba_lo)


def _gdn_kernel(q_ref, k_ref, v_ref, z_ref, ba_ref, cwq_ref, cwk_ref, cwv_ref,
                alog_ref, dtb_ref, og_ref, cum_ref, o_ref, state_ref, ext_ref, *, blk, hb):
    hg = pl.program_id(1)
    width = hb * GDN_D
    npair = blk // PAIR

    @pl.when(pl.program_id(2) == 0)
    def _():
        state_ref[...] = jnp.zeros_like(state_ref)
        ext_ref[:, 0:CONV_HALO, :] = jnp.zeros((3, CONV_HALO, width), F32)

    def conv_silu(idx, x_ref, cw_ref):
        ext_ref[idx, CONV_HALO:CONV_HALO + blk, :] = x_ref[...].astype(F32)
        y = cw_ref[CONV_WIDTH - 1:CONV_WIDTH, :] * ext_ref[idx, CONV_HALO:CONV_HALO + blk, :]
        for j in range(CONV_WIDTH - 1):
            off = CONV_HALO - (CONV_WIDTH - 1) + j
            y = y + cw_ref[j:j + 1, :] * ext_ref[idx, off:off + blk, :]
        ext_ref[idx, 0:CONV_HALO, :] = ext_ref[idx, blk:blk + CONV_HALO, :]
        return _silu(y)

    q_all = conv_silu(0, q_ref, cwq_ref)
    k_all = conv_silu(1, k_ref, cwk_ref)
    v_all = conv_silu(2, v_ref, cwv_ref)

    ba = ba_ref[...]
    lane = lax.broadcasted_iota(jnp.int32, ba.shape, 1)
    beta_all = 1.0 / (1.0 + jnp.exp(-ba))
    g_hi, g_lo = _split_bf16(-jnp.exp(alog_ref[...]) * _softplus(ba + dtb_ref[...]))
    gc_all = _dot(cum_ref[...], g_hi) + _dot(cum_ref[...], g_lo)

    row = lax.broadcasted_iota(jnp.int32, (PAIR, PAIR), 0)
    col = lax.broadcasted_iota(jnp.int32, (PAIR, PAIR), 1)
    col2 = lax.broadcasted_iota(jnp.int32, (2 * PAIR, PAIR), 1)
    same = (row & -CHUNK) == (col & -CHUNK)
    incl = same & (row >= col)
    strict = same & (row > col)

    probs = []
    qs, ks, vs, betas, gcs = [], [], [], [], []
    for hh in range(hb):
        head = hg * hb + hh
        c0 = hh * GDN_D
        qh, kh, vh = q_all[:, c0:c0 + GDN_D], k_all[:, c0:c0 + GDN_D], v_all[:, c0:c0 + GDN_D]
        qh = qh * (lax.rsqrt(jnp.sum(qh * qh, axis=-1, keepdims=True) + EPS) * (GDN_D ** -0.5))
        kh = kh * lax.rsqrt(jnp.sum(kh * kh, axis=-1, keepdims=True) + EPS)
        beta = jnp.sum(jnp.where(lane == head, beta_all, 0.0), axis=-1, keepdims=True)
        gc = jnp.sum(jnp.where(lane == head + GDN_HEADS, gc_all, 0.0), axis=-1, keepdims=True)
        for p in range(npair):
            r0 = p * PAIR
            probs.append((hh, r0))
            qs.append(qh[r0:r0 + PAIR])
            ks.append(kh[r0:r0 + PAIR])
            vs.append(vh[r0:r0 + PAIR])
            betas.append(beta[r0:r0 + PAIR])
            gcs.append(jnp.broadcast_to(gc[r0:r0 + PAIR], (PAIR, PAIR)))
    n = len(probs)
    idx = range(n)

    decays = [jnp.exp(jnp.where(incl, gcs[i] - gcs[i].T, -jnp.inf)) for i in idx]
    kbs = [ks[i] * betas[i] for i in idx]
    kbf = [ks[i].astype(BF16) for i in idx]
    a_lows = [jnp.where(strict, _dot_nt(kbs[i].astype(BF16), kbf[i]) * decays[i], 0.0) for i in idx]
    attns = [(_dot_nt(qs[i].astype(BF16), kbf[i]) * decays[i]).astype(BF16) for i in idx]

    eye = jnp.where(row == col, 1.0, 0.0)
    ts = [eye - jnp.where((row - col == 1) & ((row & 1) == 1), a_lows[i], 0.0) for i in idx]
    s = 2
    while s < CHUNK:
        sub = ((row & -(2 * s)) == (col & -(2 * s))) & ((row & s) != 0) & ((col & s) == 0)
        tbs = [ts[i].astype(BF16) for i in idx]
        xs = [_dot(jnp.where(sub, a_lows[i], 0.0).astype(BF16), tbs[i]).astype(BF16) for i in idx]
        ts = [ts[i] - _dot(tbs[i], xs[i]) for i in idx]
        s *= 2

    egs = [jnp.exp(gcs[i]) for i in idx]
    rhs = [jnp.concatenate([vs[i] * betas[i], kbs[i] * egs[i]], axis=-1).astype(BF16) for i in idx]
    sols = [_dot(ts[i].astype(BF16), rhs[i]) for i in idx]
    aws = [_dot(attns[i], sols[i].astype(BF16)) for i in idx]
    qps = [(qs[i] * egs[i] - aws[i][:, GDN_D:]).astype(BF16) for i in idx]
    g_last = [[gcs[i][c * CHUNK + CHUNK - 1:(c + 1) * CHUNK, :] for c in range(2)] for i in idx]
    kds = [(ks[i] * jnp.exp(jnp.where(row < CHUNK, g_last[i][0], g_last[i][1]) - gcs[i])).astype(BF16)
           for i in idx]
    sol_ts = [sols[i].T for i in idx]
    pns = [[_dot(jnp.where((col2 & CHUNK) == c * CHUNK, sol_ts[i], 0.0).astype(BF16), kds[i])
            for c in range(2)] for i in idx]

    states = [state_ref[hh] for hh in range(hb)]
    inters = [[None, None] for _ in idx]
    for p in range(npair):
        for c in range(2):
            for hh in range(hb):
                i = hh * npair + p
                st = states[hh]
                st_bf = st.astype(BF16)
                inters[i][c] = _dot_nt(qps[i][c * CHUNK:(c + 1) * CHUNK], st_bf)
                pn = pns[i][c]
                states[hh] = (st * jnp.exp(g_last[i][c]) - _dot(st_bf, pn[GDN_D:].astype(BF16))
                              + pn[:GDN_D])
    for hh in range(hb):
        state_ref[hh] = states[hh]

    og = og_ref[...]
    for i in idx:
        hh, r0 = probs[i]
        c0 = hh * GDN_D
        o = jnp.concatenate(inters[i], axis=0) + aws[i][:, :GDN_D]
        zg = z_ref[r0:r0 + PAIR, c0:c0 + GDN_D].astype(F32)
        o = o * _rms_scale(o) * og * _silu(zg)
        o_ref[r0:r0 + PAIR, c0:c0 + GDN_D] = o.astype(o_ref.dtype)


def _gdn_mixer(proj3, ba3, conv_w8, alog_pad, dtb_pad, o_gain, cum_mat, *, blk, hb):
    b, s, _ = proj3.shape
    ng = GDN_HEADS // hb
    width = hb * GDN_D
    tok = lambda off: pl.BlockSpec((None, blk, width), lambda bi, gi, li: (bi, li, gi + off))
    cw = lambda off: pl.BlockSpec((8, width), lambda bi, gi, li: (0, gi + off))
    vec = pl.BlockSpec((1, LANES), lambda bi, gi, li: (0, 0))
    return pl.pallas_call(
        functools.partial(_gdn_kernel, blk=blk, hb=hb),
        out_shape=jax.ShapeDtypeStruct((b, s, GDN_HEADS * GDN_D), BF16),
        grid=(b, ng, s // blk),
        in_specs=[tok(0), tok(ng), tok(2 * ng), tok(3 * ng),
                  pl.BlockSpec((None, blk, LANES), lambda bi, gi, li: (bi, li, 0)),
                  cw(0), cw(ng), cw(2 * ng), vec, vec, vec,
                  pl.BlockSpec((blk, blk), lambda bi, gi, li: (0, 0))],
        out_specs=pl.BlockSpec((None, blk, width), lambda bi, gi, li: (bi, li, gi)),
        scratch_shapes=[pltpu.VMEM((hb, GDN_D, GDN_D), F32),
                        pltpu.VMEM((3, blk + CONV_HALO, width), F32)],
        compiler_params=pltpu.CompilerParams(
            dimension_semantics=("parallel", "parallel", "arbitrary"), vmem_limit_bytes=VMEM_LIMIT),
        name="gdn_mixer",
    )(proj3, proj3, proj3, proj3, ba3, conv_w8, conv_w8, conv_w8, alog_pad, dtb_pad, o_gain, cum_mat)


def _proj_ffn_kernel(a_ref, wo_ref, res_ref, g_ref, wgu_ref, wd_ref, o_ref, mid_ref):
    f = wd_ref.shape[0]
    h1 = res_ref[...] + _dot(a_ref[...], wo_ref[...])
    o_ref[...] = h1
    xn = (h1 * _rms_scale(h1) * g_ref[...]).astype(BF16)
    for c0 in range(0, f, FF_CHUNK):
        gate = _dot(xn, wgu_ref[:, c0:c0 + FF_CHUNK])
        up = _dot(xn, wgu_ref[:, f + c0:f + c0 + FF_CHUNK])
        mid_ref[:, c0:c0 + FF_CHUNK] = (_silu(gate) * up).astype(BF16)
    o_ref[...] += _dot(mid_ref[...], wd_ref[...])


def _proj_ffn(a2, w_out, res2, gain, w_gu, w_down, *, tm):
    t, d = res2.shape
    ka = a2.shape[1]
    f = w_down.shape[0]
    return pl.pallas_call(
        _proj_ffn_kernel,
        out_shape=jax.ShapeDtypeStruct((t, d), F32),
        grid=(t // tm,),
        in_specs=[pl.BlockSpec((tm, ka), lambda i: (i, 0)),
                  _resident((ka, d)),
                  pl.BlockSpec((tm, d), lambda i: (i, 0)),
                  _resident((1, d)), _resident((d, 2 * f)), _resident((f, d))],
        out_specs=pl.BlockSpec((tm, d), lambda i: (i, 0)),
        scratch_shapes=[pltpu.VMEM((tm, f), BF16)],
        compiler_params=pltpu.CompilerParams(
            dimension_semantics=("parallel",), vmem_limit_bytes=VMEM_LIMIT),
        name="proj_ffn",
    )(a2, w_out, res2, gain, w_gu, w_down)


def _sb_qkv_kernel(h_ref, gq_ref, gkv_ref, w_ref, qg_ref, kg_ref, o_ref):
    tm = h_ref.shape[0]
    nq = SB_HEADS * SB_DH

    def head_norm(rows, c0, y, gain):
        for c in range(0, y.shape[1], SB_DH):
            yc = y[:, c:c + SB_DH]
            o_ref[rows, c0 + c:c0 + c + SB_DH] = (yc * _rms_scale(yc) * gain).astype(o_ref.dtype)

    for r0 in range(0, tm, ROW_CHUNK):
        rows = slice(r0, r0 + ROW_CHUNK)
        x = h_ref[rows, :]
        xs = x * _rms_scale(x)
        xq = (xs * gq_ref[...]).astype(BF16)
        xkv = (xs * gkv_ref[...]).astype(BF16)
        head_norm(rows, 0, _dot(xq, w_ref[:, 0:nq]), qg_ref[...] * (SB_DH ** -0.5 * LOG2E))
        head_norm(rows, nq, _dot(xkv, w_ref[:, nq:2 * nq]), kg_ref[...])
        o_ref[rows, 2 * nq:3 * nq] = _dot(xkv, w_ref[:, 2 * nq:3 * nq]).astype(o_ref.dtype)


def _sb_qkv(h2, gain_q, gain_kv, w_qkv, q_gain, k_gain, *, tm):
    t, d = h2.shape
    n = w_qkv.shape[1]
    return pl.pallas_call(
        _sb_qkv_kernel,
        out_shape=jax.ShapeDtypeStruct((t, n), BF16),
        grid=(t // tm,),
        in_specs=[pl.BlockSpec((tm, d), lambda i: (i, 0)), _resident((1, d)), _resident((1, d)),
                  _resident((d, n)), _resident((1, SB_DH)), _resident((1, SB_DH))],
        out_specs=pl.BlockSpec((tm, n), lambda i: (i, 0)),
        compiler_params=pltpu.CompilerParams(
            dimension_semantics=("parallel",), vmem_limit_bytes=VMEM_LIMIT),
        name="sb_qkv",
    )(h2, gain_q, gain_kv, w_qkv, q_gain, k_gain)


def _sb_attn_kernel(q_ref, k_ref, v_ref, suf_ref, o_ref, *, tq, tk, hb):
    nd = tq // tk
    qi = pl.program_id(2)
    suf = suf_ref[...]
    ahead = (lax.broadcasted_iota(jnp.int32, (tq, tk), 0)
             - lax.broadcasted_iota(jnp.int32, (tq, tk), 1))
    heads = [slice(hh * SB_DH, (hh + 1) * SB_DH) for hh in range(hb)]
    qs = [q_ref[:, hs] for hs in heads]

    def scores(hh, j):
        z = _dot_nt(qs[hh], k_ref[pl.ds(pl.multiple_of(j * tk, tk), tk), heads[hh]])
        return z, jnp.maximum(z, 0.0) + jnp.log2(1.0 + jnp.exp2(-jnp.abs(z)))

    def absorb(hh, j, z, sp, rest, acc, causal):
        fail = sp if causal is None else jnp.where(causal, sp, 0.0)
        after = _dot(fail.astype(BF16), suf) + rest
        att = jnp.exp2((z - sp) - after)
        if causal is not None:
            att = jnp.where(causal, att, 0.0)
        vb = v_ref[pl.ds(pl.multiple_of(j * tk, tk), tk), heads[hh]]
        return rest + jnp.sum(fail, axis=-1, keepdims=True), acc + _dot(att.astype(BF16), vb)

    diag = list(range(nd - 1, -1, -1))
    zs = [[scores(hh, nd * qi + d) for d in diag] for hh in range(hb)]
    carry = []
    for hh in range(hb):
        rest, acc = jnp.zeros((tq, 1), F32), jnp.zeros((tq, SB_DH), F32)
        for i, d in enumerate(diag):
            rest, acc = absorb(hh, nd * qi + d, *zs[hh][i], rest, acc, ahead > d * tk)
        carry += [rest, acc]

    def low_water(carry):
        return functools.reduce(jnp.minimum, [jnp.min(carry[2 * hh]) for hh in range(hb)])

    def cond(state):
        it, low = state[0], state[1]
        return (it < nd * qi) & (low < DEAD_LOG2)

    def body(state):
        it, carry = state[0], list(state[2:])
        j = nd * qi - 1 - it
        zs = [scores(hh, j) for hh in range(hb)]
        for hh in range(hb):
            carry[2 * hh], carry[2 * hh + 1] = absorb(hh, j, *zs[hh], carry[2 * hh], carry[2 * hh + 1], None)
        return (it + 1, low_water(carry), *carry)

    final = lax.while_loop(cond, body, (jnp.int32(0), low_water(carry), *carry))
    for hh in range(hb):
        o_ref[:, heads[hh]] = final[3 + 2 * hh].astype(o_ref.dtype)


def _sb_attn(qkv3, suffix_mat, *, tq, tk, hb):
    b, s, _ = qkv3.shape
    ng = SB_HEADS // hb
    width = hb * SB_DH
    return pl.pallas_call(
        functools.partial(_sb_attn_kernel, tq=tq, tk=tk, hb=hb),
        out_shape=jax.ShapeDtypeStruct((b, s, SB_HEADS * SB_DH), BF16),
        grid=(b, ng, s // tq),
        in_specs=[pl.BlockSpec((None, tq, width), lambda bi, gi, qi: (bi, qi, gi)),
                  pl.BlockSpec((None, s, width), lambda bi, gi, qi: (bi, 0, gi + ng)),
                  pl.BlockSpec((None, s, width), lambda bi, gi, qi: (bi, 0, gi + 2 * ng)),
                  pl.BlockSpec((tk, tk), lambda bi, gi, qi: (0, 0))],
        out_specs=pl.BlockSpec((None, tq, width), lambda bi, gi, qi: (bi, qi, gi)),
        compiler_params=pltpu.CompilerParams(
            dimension_semantics=("parallel", "parallel", "arbitrary"), vmem_limit_bytes=VMEM_LIMIT),
        name="sb_attn",
    )(qkv3, qkv3, qkv3, suffix_mat)


def _pick(n, prefs):
    for p in prefs:
        if n % p == 0:
            return p
    return n


def kernel(x, attn_norm, ffn_norm, gdn_w_in, gdn_conv_w, gdn_a_log, gdn_dt_bias, gdn_o_gain, gdn_w_out,
           kv_norm, w_kv, k_gain, sb_w_q, sb_q_gain, sb_w_out, ffn_w_gu, ffn_w_down):
    b, s, d = x.shape
    t = b * s
    hd = GDN_HEADS * GDN_D
    n_main = 4 * hd
    tm = _pick(t, (1024, ROW_CHUNK))
    tm_ffn = ROW_CHUNK
    gdn_blk = _pick(s, (256, 128))
    tk = _pick(s, (256, 128))

    row = lambda vec: vec.reshape(1, -1).astype(F32)
    x2 = x.reshape(t, d)

    w_in = gdn_w_in[0]
    w_ba = jnp.pad(w_in[:, n_main:], ((0, 0), (0, LANES - 2 * GDN_HEADS)))
    wba_hi = w_ba.astype(BF16)
    wba_lo = (w_ba - wba_hi.astype(F32)).astype(BF16)
    proj, ba = _gdn_inproj(x2, row(attn_norm[0]), w_in[:, :n_main].astype(BF16), wba_hi, wba_lo, tm=tm)
    lane_pad = lambda vec: jnp.pad(vec.astype(F32), (GDN_HEADS, LANES - 2 * GDN_HEADS)).reshape(1, LANES)
    conv_w8 = jnp.pad(gdn_conv_w[0].astype(F32), ((0, 8 - CONV_WIDTH), (0, 0)))
    tok = jnp.arange(gdn_blk)
    cum_mat = ((tok[:, None] // CHUNK == tok[None, :] // CHUNK) & (tok[:, None] >= tok[None, :])).astype(BF16)
    o_gdn = _gdn_mixer(proj.reshape(b, s, n_main), ba.reshape(b, s, LANES), conv_w8,
                       lane_pad(gdn_a_log[0]), lane_pad(gdn_dt_bias[0]), row(gdn_o_gain[0]), cum_mat,
                       blk=gdn_blk, hb=8)
    h2 = _proj_ffn(o_gdn.reshape(t, hd), gdn_w_out[0].astype(BF16), x2, row(ffn_norm[0]),
                   ffn_w_gu[0].astype(BF16), ffn_w_down[0].astype(BF16), tm=tm_ffn)

    w_qkv = jnp.concatenate([sb_w_q[0], w_kv], axis=1).astype(BF16)
    qkv = _sb_qkv(h2, row(attn_norm[1]), row(kv_norm), w_qkv, row(sb_q_gain[0]), row(k_gain), tm=tm)
    idx = jnp.arange(tk)
    suffix_mat = (idx[:, None] > idx[None, :]).astype(BF16)
    o_sb = _sb_attn(qkv.reshape(b, s, 3 * SB_HEADS * SB_DH), suffix_mat, tq=tk, tk=tk, hb=4)
    h2 = _proj_ffn(o_sb.reshape(t, SB_HEADS * SB_DH), sb_w_out[0].astype(BF16), h2, row(ffn_norm[1]),
                   ffn_w_gu[1].astype(BF16), ffn_w_down[1].astype(BF16), tm=tm_ffn)
    return h2.reshape(b, s, d)
```

```python
import functools

import jax
import jax.numpy as jnp
from jax import lax
from jax.experimental import pallas as pl
from jax.experimental.pallas import tpu as pltpu

F32 = jnp.float32
BF16 = jnp.bfloat16
EPS = 1e-6
LOG2E = 1.4426950408889634

LANES = 128
GDN_HEADS = 8
GDN_D = 128
CHUNK = 64
PAIR = 2 * CHUNK
CONV_WIDTH = 4
CONV_HALO = 8
SB_HEADS = 8
SB_DH = 128
VMEM_LIMIT = 56 * 1024 * 1024
ROW_CHUNK = 512
COL_CHUNK = 1024
FF_CHUNK = 256
DEAD_LOG2 = 160.0


def _dot(a, b):
    return jnp.dot(a, b, preferred_element_type=F32)


def _dot_nt(a, b):
    return lax.dot_general(a, b, (((1,), (1,)), ((), ())), preferred_element_type=F32)


def _silu(x):
    half = 0.5 * x
    return half + half * jnp.tanh(half)


def _softplus(x):
    return jnp.maximum(x, 0.0) + jnp.log(1.0 + jnp.exp(-jnp.abs(x)))


def _split_bf16(x):
    hi = x.astype(BF16)
    lo = (x - hi.astype(F32)).astype(BF16)
    return hi, lo


def _rms_scale(x):
    return lax.rsqrt(jnp.mean(x * x, axis=-1, keepdims=True) + EPS)


def _resident(shape):
    return pl.BlockSpec(shape, lambda *_: (0,) * len(shape), pipeline_mode=pl.Buffered(1))


def _inproj_kernel(x_ref, g_ref, w_ref, wba_ref, y_ref, ba_ref):
    tm, n = y_ref.shape
    for r0 in range(0, tm, ROW_CHUNK):
        rows = slice(r0, r0 + ROW_CHUNK)
        x = x_ref[rows, :]
        xh, xl = _split_bf16(x * _rms_scale(x) * g_ref[...])
        both = _dot(xh, wba_ref[...])
        ba_ref[rows, :] = both[:, :LANES] + both[:, LANES:] + _dot(xl, wba_ref[:, :LANES])
        for c0 in range(0, n, COL_CHUNK):
            cols = slice(c0, c0 + COL_CHUNK)
            y_ref[rows, cols] = _dot(xh, w_ref[:, cols]).astype(y_ref.dtype)


def _gdn_inproj(x2, gain, w_main, wba, *, tm):
    t, d = x2.shape
    n = w_main.shape[1]
    return pl.pallas_call(
        _inproj_kernel,
        out_shape=(jax.ShapeDtypeStruct((t, n), BF16),
                   jax.ShapeDtypeStruct((t, LANES), F32)),
        grid=(t // tm,),
        in_specs=[pl.BlockSpec((tm, d), lambda i: (i, 0)),
                  _resident((1, d)), _resident((d, n)), _resident((d, 2 * LANES))],
        out_specs=(pl.BlockSpec((tm, n), lambda i: (i, 0)),
                   pl.BlockSpec((tm, LANES), lambda i: (i, 0))),
        compiler_params=pltpu.CompilerParams(
            dimension_semantics=("parallel",), vmem_limit_bytes=VMEM_LIMIT),
        name="gdn_inproj",
    )(x2, gain, w_main, wba)


def _gdn_kernel(q_ref, k_ref, v_ref, z_ref, ba_ref, cwq_ref, cwk_ref, cwv_ref,
                alog_ref, dtb_ref, og_ref, cum_ref, o_ref, state_ref, ext_ref, *, blk, hb):
    hg = pl.program_id(1)
    width = hb * GDN_D
    npair = blk // PAIR

    @pl.when(pl.program_id(2) == 0)
    def _():
        state_ref[...] = jnp.zeros_like(state_ref)
        ext_ref[:, 0:CONV_HALO, :] = jnp.zeros((3, CONV_HALO, width), F32)

    def conv_silu(idx, x_ref, cw_ref):
        ext_ref[idx, CONV_HALO:CONV_HALO + blk, :] = x_ref[...].astype(F32)
        y = cw_ref[CONV_WIDTH - 1:CONV_WIDTH, :] * ext_ref[idx, CONV_HALO:CONV_HALO + blk, :]
        for j in range(CONV_WIDTH - 1):
            off = CONV_HALO - (CONV_WIDTH - 1) + j
            y = y + cw_ref[j:j + 1, :] * ext_ref[idx, off:off + blk, :]
        ext_ref[idx, 0:CONV_HALO, :] = ext_ref[idx, blk:blk + CONV_HALO, :]
        return _silu(y)

    q_all = conv_silu(0, q_ref, cwq_ref)
    k_all = conv_silu(1, k_ref, cwk_ref)
    v_all = conv_silu(2, v_ref, cwv_ref)

    ba = ba_ref[...]
    lane = lax.broadcasted_iota(jnp.int32, ba.shape, 1)
    beta_all = 1.0 / (1.0 + jnp.exp(-ba))
    g_hi, g_lo = _split_bf16(-jnp.exp(alog_ref[...]) * _softplus(ba + dtb_ref[...]))
    gc_all = _dot(cum_ref[...], g_hi) + _dot(cum_ref[...], g_lo)

    row = lax.broadcasted_iota(jnp.int32, (PAIR, PAIR), 0)
    col = lax.broadcasted_iota(jnp.int32, (PAIR, PAIR), 1)
    col2 = lax.broadcasted_iota(jnp.int32, (2 * PAIR, PAIR), 1)
    same = (row & -CHUNK) == (col & -CHUNK)
    incl = same & (row >= col)
    strict = same & (row > col)

    probs = []
    qs, ks, vs, betas, gcs = [], [], [], [], []
    for hh in range(hb):
        head = hg * hb + hh
        c0 = hh * GDN_D
        qh, kh, vh = q_all[:, c0:c0 + GDN_D], k_all[:, c0:c0 + GDN_D], v_all[:, c0:c0 + GDN_D]
        qh = qh * (lax.rsqrt(jnp.sum(qh * qh, axis=-1, keepdims=True) + EPS) * (GDN_D ** -0.5))
        kh = kh * lax.rsqrt(jnp.sum(kh * kh, axis=-1, keepdims=True) + EPS)
        beta = jnp.sum(jnp.where(lane == head, beta_all, 0.0), axis=-1, keepdims=True)
        gc = jnp.sum(jnp.where(lane == head + GDN_HEADS, gc_all, 0.0), axis=-1, keepdims=True)
        for p in range(npair):
            r0 = p * PAIR
            probs.append((hh, r0))
            qs.append(qh[r0:r0 + PAIR])
            ks.append(kh[r0:r0 + PAIR])
            vs.append(vh[r0:r0 + PAIR])
            betas.append(beta[r0:r0 + PAIR])
            gcs.append(jnp.broadcast_to(gc[r0:r0 + PAIR], (PAIR, PAIR)))
    n = len(probs)
    idx = range(n)

    decays = [jnp.exp(jnp.where(incl, gcs[i] - gcs[i].T, -jnp.inf)) for i in idx]
    kbs = [ks[i] * betas[i] for i in idx]
    kbf = [ks[i].astype(BF16) for i in idx]
    a_lows = [jnp.where(strict, _dot_nt(kbs[i].astype(BF16), kbf[i]) * decays[i], 0.0) for i in idx]
    attns = [(_dot_nt(qs[i].astype(BF16), kbf[i]) * decays[i]).astype(BF16) for i in idx]

    eye = jnp.where(row == col, 1.0, 0.0)
    ts = [eye - jnp.where((row - col == 1) & ((row & 1) == 1), a_lows[i], 0.0) for i in idx]
    s = 2
    while s < CHUNK:
        sub = ((row & -(2 * s)) == (col & -(2 * s))) & ((row & s) != 0) & ((col & s) == 0)
        tbs = [ts[i].astype(BF16) for i in idx]
        xs = [_dot(jnp.where(sub, a_lows[i], 0.0).astype(BF16), tbs[i]).astype(BF16) for i in idx]
        ts = [ts[i] - _dot(tbs[i], xs[i]) for i in idx]
        s *= 2

    egs = [jnp.exp(gcs[i]) for i in idx]
    rhs = [jnp.concatenate([vs[i] * betas[i], kbs[i] * egs[i]], axis=-1).astype(BF16) for i in idx]
    sols = [_dot(ts[i].astype(BF16), rhs[i]) for i in idx]
    aws = [_dot(attns[i], sols[i].astype(BF16)) for i in idx]
    qps = [(qs[i] * egs[i] - aws[i][:, GDN_D:]).astype(BF16) for i in idx]
    g_last = [[gcs[i][c * CHUNK + CHUNK - 1:(c + 1) * CHUNK, :] for c in range(2)] for i in idx]
    kds = [(ks[i] * jnp.exp(jnp.where(row < CHUNK, g_last[i][0], g_last[i][1]) - gcs[i])).astype(BF16)
           for i in idx]
    sol_ts = [sols[i].T for i in idx]
    pns = [[_dot(jnp.where((col2 & CHUNK) == c * CHUNK, sol_ts[i], 0.0).astype(BF16), kds[i])
            for c in range(2)] for i in idx]

    states = [state_ref[hh] for hh in range(hb)]
    inters = [[None, None] for _ in idx]
    for p in range(npair):
        for c in range(2):
            for hh in range(hb):
                i = hh * npair + p
                st = states[hh]
                st_bf = st.astype(BF16)
                inters[i][c] = _dot_nt(qps[i][c * CHUNK:(c + 1) * CHUNK], st_bf)
                pn = pns[i][c]
                states[hh] = (st * jnp.exp(g_last[i][c]) - _dot(st_bf, pn[GDN_D:].astype(BF16))
                              + pn[:GDN_D])
    for hh in range(hb):
        state_ref[hh] = states[hh]

    og = og_ref[...]
    for i in idx:
        hh, r0 = probs[i]
        c0 = hh * GDN_D
        o = jnp.concatenate(inters[i], axis=0) + aws[i][:, :GDN_D]
        zg = z_ref[r0:r0 + PAIR, c0:c0 + GDN_D].astype(F32)
        o = o * _rms_scale(o) * og * _silu(zg)
        o_ref[r0:r0 + PAIR, c0:c0 + GDN_D] = o.astype(o_ref.dtype)


def _gdn_mixer(proj3, ba3, conv_w8, alog_pad, dtb_pad, o_gain, cum_mat, *, blk, hb):
    b, s, _ = proj3.shape
    ng = GDN_HEADS // hb
    width = hb * GDN_D
    tok = lambda off: pl.BlockSpec((None, blk, width), lambda bi, gi, li: (bi, li, gi + off))
    cw = lambda off: pl.BlockSpec((8, width), lambda bi, gi, li: (0, gi + off))
    vec = pl.BlockSpec((1, LANES), lambda bi, gi, li: (0, 0))
    return pl.pallas_call(
        functools.partial(_gdn_kernel, blk=blk, hb=hb),
        out_shape=jax.ShapeDtypeStruct((b, s, GDN_HEADS * GDN_D), BF16),
        grid=(b, ng, s // blk),
        in_specs=[tok(0), tok(ng), tok(2 * ng), tok(3 * ng),
                  pl.BlockSpec((None, blk, LANES), lambda bi, gi, li: (bi, li, 0)),
                  cw(0), cw(ng), cw(2 * ng), vec, vec, vec,
                  pl.BlockSpec((blk, blk), lambda bi, gi, li: (0, 0))],
        out_specs=pl.BlockSpec((None, blk, width), lambda bi, gi, li: (bi, li, gi)),
        scratch_shapes=[pltpu.VMEM((hb, GDN_D, GDN_D), F32),
                        pltpu.VMEM((3, blk + CONV_HALO, width), F32)],
        compiler_params=pltpu.CompilerParams(
            dimension_semantics=("parallel", "parallel", "arbitrary"), vmem_limit_bytes=VMEM_LIMIT),
        name="gdn_mixer",
    )(proj3, proj3, proj3, proj3, ba3, conv_w8, conv_w8, conv_w8, alog_pad, dtb_pad, o_gain, cum_mat)


def _proj_ffn_kernel(a_ref, wo_ref, res_ref, g_ref, wgu_ref, wd_ref, o_ref, mid_ref):
    f = wd_ref.shape[0]
    h1 = res_ref[...] + _dot(a_ref[...], wo_ref[...])
    o_ref[...] = h1
    xn = (h1 * _rms_scale(h1) * g_ref[...]).astype(BF16)
    for c0 in range(0, f, FF_CHUNK):
        gate = _dot(xn, wgu_ref[:, c0:c0 + FF_CHUNK])
        up = _dot(xn, wgu_ref[:, f + c0:f + c0 + FF_CHUNK])
        mid_ref[:, c0:c0 + FF_CHUNK] = (_silu(gate) * up).astype(BF16)
    o_ref[...] += _dot(mid_ref[...], wd_ref[...])


def _proj_ffn(a2, w_out, res2, gain, w_gu, w_down, *, tm):
    t, d = res2.shape
    ka = a2.shape[1]
    f = w_down.shape[0]
    return pl.pallas_call(
        _proj_ffn_kernel,
        out_shape=jax.ShapeDtypeStruct((t, d), F32),
        grid=(t // tm,),
        in_specs=[pl.BlockSpec((tm, ka), lambda i: (i, 0)),
                  _resident((ka, d)),
                  pl.BlockSpec((tm, d), lambda i: (i, 0)),
                  _resident((1, d)), _resident((d, 2 * f)), _resident((f, d))],
        out_specs=pl.BlockSpec((tm, d), lambda i: (i, 0)),
        scratch_shapes=[pltpu.VMEM((tm, f), BF16)],
        compiler_params=pltpu.CompilerParams(
            dimension_semantics=("parallel",), vmem_limit_bytes=VMEM_LIMIT),
        name="proj_ffn",
    )(a2, w_out, res2, gain, w_gu, w_down)


def _sb_qkv_kernel(h_ref, gq_ref, gkv_ref, wq_ref, wkv_ref, qg_ref, kg_ref, o_ref):
    tm = h_ref.shape[0]
    nq = SB_HEADS * SB_DH

    def head_norm(rows, c0, y, gain):
        for c in range(0, y.shape[1], SB_DH):
            yc = y[:, c:c + SB_DH]
            o_ref[rows, c0 + c:c0 + c + SB_DH] = (yc * _rms_scale(yc) * gain).astype(o_ref.dtype)

    for r0 in range(0, tm, ROW_CHUNK):
        rows = slice(r0, r0 + ROW_CHUNK)
        x = h_ref[rows, :]
        xs = x * _rms_scale(x)
        xq = (xs * gq_ref[...]).astype(BF16)
        xkv = (xs * gkv_ref[...]).astype(BF16)
        head_norm(rows, 0, _dot(xq, wq_ref[...]), qg_ref[...] * (SB_DH ** -0.5 * LOG2E))
        head_norm(rows, nq, _dot(xkv, wkv_ref[:, 0:nq]), kg_ref[...])
        o_ref[rows, 2 * nq:3 * nq] = _dot(xkv, wkv_ref[:, nq:2 * nq]).astype(o_ref.dtype)


def _sb_qkv(h2, gain_q, gain_kv, w_q, w_kv, q_gain, k_gain, *, tm):
    t, d = h2.shape
    n = w_q.shape[1] + w_kv.shape[1]
    return pl.pallas_call(
        _sb_qkv_kernel,
        out_shape=jax.ShapeDtypeStruct((t, n), BF16),
        grid=(t // tm,),
        in_specs=[pl.BlockSpec((tm, d), lambda i: (i, 0)), _resident((1, d)), _resident((1, d)),
                  _resident(w_q.shape), _resident(w_kv.shape),
                  _resident((1, SB_DH)), _resident((1, SB_DH))],
        out_specs=pl.BlockSpec((tm, n), lambda i: (i, 0)),
        compiler_params=pltpu.CompilerParams(
            dimension_semantics=("parallel",), vmem_limit_bytes=VMEM_LIMIT),
        name="sb_qkv",
    )(h2, gain_q, gain_kv, w_q, w_kv, q_gain, k_gain)


def _sb_attn_kernel(q_ref, k_ref, v_ref, suf_ref, o_ref, *, tq, tk, hb):
    nd = tq // tk
    qi = pl.program_id(2)
    suf = suf_ref[...]
    ahead = (lax.broadcasted_iota(jnp.int32, (tq, tk), 0)
             - lax.broadcasted_iota(jnp.int32, (tq, tk), 1))
    heads = [slice(hh * SB_DH, (hh + 1) * SB_DH) for hh in range(hb)]
    qs = [q_ref[:, hs] for hs in heads]

    def scores(hh, j):
        z = _dot_nt(qs[hh], k_ref[pl.ds(pl.multiple_of(j * tk, tk), tk), heads[hh]])
        return z, jnp.maximum(z, 0.0) + jnp.log2(1.0 + jnp.exp2(-jnp.abs(z)))

    def absorb(hh, j, z, sp, rest, acc, causal):
        fail = sp if causal is None else jnp.where(causal, sp, 0.0)
        after = _dot(fail.astype(BF16), suf) + rest
        att = jnp.exp2((z - sp) - after)
        if causal is not None:
            att = jnp.where(causal, att, 0.0)
        vb = v_ref[pl.ds(pl.multiple_of(j * tk, tk), tk), heads[hh]]
        return rest + jnp.sum(fail, axis=-1, keepdims=True), acc + _dot(att.astype(BF16), vb)

    diag = list(range(nd - 1, -1, -1))
    has_prev = qi > 0
    prev = jnp.maximum(nd * qi - 1, 0)
    zs = [[scores(hh, nd * qi + d) for d in diag] + [scores(hh, prev)] for hh in range(hb)]
    carry = []
    for hh in range(hb):
        rest, acc = jnp.zeros((tq, 1), F32), jnp.zeros((tq, SB_DH), F32)
        for i, d in enumerate(diag):
            rest, acc = absorb(hh, nd * qi + d, *zs[hh][i], rest, acc, ahead > d * tk)
        rest, acc = absorb(hh, prev, *zs[hh][nd], rest, acc, has_prev)
        carry += [rest, acc]

    def low_water(carry):
        return functools.reduce(jnp.minimum, [jnp.min(carry[2 * hh]) for hh in range(hb)])

    def cond(state):
        it, low = state[0], state[1]
        return (it < nd * qi - 1) & (low < DEAD_LOG2)

    def body(state):
        it, carry = state[0], list(state[2:])
        j = nd * qi - 2 - it
        zs = [scores(hh, j) for hh in range(hb)]
        for hh in range(hb):
            carry[2 * hh], carry[2 * hh + 1] = absorb(hh, j, *zs[hh], carry[2 * hh], carry[2 * hh + 1], None)
        return (it + 1, low_water(carry), *carry)

    final = lax.while_loop(cond, body, (jnp.int32(0), low_water(carry), *carry))
    for hh in range(hb):
        o_ref[:, heads[hh]] = final[3 + 2 * hh].astype(o_ref.dtype)


def _sb_attn(qkv3, suffix_mat, *, tq, tk, hb):
    b, s, _ = qkv3.shape
    ng = SB_HEADS // hb
    width = hb * SB_DH
    return pl.pallas_call(
        functools.partial(_sb_attn_kernel, tq=tq, tk=tk, hb=hb),
        out_shape=jax.ShapeDtypeStruct((b, s, SB_HEADS * SB_DH), BF16),
        grid=(b, ng, s // tq),
        in_specs=[pl.BlockSpec((None, tq, width), lambda bi, gi, qi: (bi, qi, gi)),
                  pl.BlockSpec((None, s, width), lambda bi, gi, qi: (bi, 0, gi + ng)),
                  pl.BlockSpec((None, s, width), lambda bi, gi, qi: (bi, 0, gi + 2 * ng)),
                  pl.BlockSpec((tk, tk), lambda bi, gi, qi: (0, 0))],
        out_specs=pl.BlockSpec((None, tq, width), lambda bi, gi, qi: (bi, qi, gi)),
        compiler_params=pltpu.CompilerParams(
            dimension_semantics=("parallel", "parallel", "arbitrary"), vmem_limit_bytes=VMEM_LIMIT),
        name="sb_attn",
    )(qkv3, qkv3, qkv3, suffix_mat)


def _pick(n, prefs):
    for p in prefs:
        if n % p == 0:
            return p
    return n


def kernel(x, attn_norm, ffn_norm, gdn_w_in, gdn_conv_w, gdn_a_log, gdn_dt_bias, gdn_o_gain, gdn_w_out,
           kv_norm, w_kv, k_gain, sb_w_q, sb_q_gain, sb_w_out, ffn_w_gu, ffn_w_down):
    b, s, d = x.shape
    t = b * s
    hd = GDN_HEADS * GDN_D
    n_main = 4 * hd
    tm = _pick(t, (1024, ROW_CHUNK))
    tm_ffn = ROW_CHUNK
    gdn_blk = _pick(s, (256, 128))
    tk = _pick(s, (256, 128))

    row = lambda vec: vec.reshape(1, -1).astype(F32)
    x2 = x.reshape(t, d)

    w_in = gdn_w_in[0]
    w_ba = jnp.pad(w_in[:, n_main:], ((0, 0), (0, LANES - 2 * GDN_HEADS)))
    wba_hi = w_ba.astype(BF16)
    wba_lo = (w_ba - wba_hi.astype(F32)).astype(BF16)
    proj, ba = _gdn_inproj(x2, row(attn_norm[0]), w_in[:, :n_main].astype(BF16),
                           jnp.concatenate([wba_hi, wba_lo], axis=1), tm=tm)
    lane_pad = lambda vec: jnp.pad(vec.astype(F32), (GDN_HEADS, LANES - 2 * GDN_HEADS)).reshape(1, LANES)
    conv_w8 = jnp.pad(gdn_conv_w[0].astype(F32), ((0, 8 - CONV_WIDTH), (0, 0)))
    tok = jnp.arange(gdn_blk)
    cum_mat = ((tok[:, None] // CHUNK == tok[None, :] // CHUNK) & (tok[:, None] >= tok[None, :])).astype(BF16)
    o_gdn = _gdn_mixer(proj.reshape(b, s, n_main), ba.reshape(b, s, LANES), conv_w8,
                       lane_pad(gdn_a_log[0]), lane_pad(gdn_dt_bias[0]), row(gdn_o_gain[0]), cum_mat,
                       blk=gdn_blk, hb=8)
    h2 = _proj_ffn(o_gdn.reshape(t, hd), gdn_w_out[0].astype(BF16), x2, row(ffn_norm[0]),
                   ffn_w_gu[0].astype(BF16), ffn_w_down[0].astype(BF16), tm=tm_ffn)

    qkv = _sb_qkv(h2, row(attn_norm[1]), row(kv_norm), sb_w_q[0].astype(BF16), w_kv.astype(BF16),
                  row(sb_q_gain[0]), row(k_gain), tm=tm)
    idx = jnp.arange(tk)
    suffix_mat = (idx[:, None] > idx[None, :]).astype(BF16)
    o_sb = _sb_attn(qkv.reshape(b, s, 3 * SB_HEADS * SB_DH), suffix_mat, tq=tk, tk=tk, hb=4)
    h2 = _proj_ffn(o_sb.reshape(t, SB_HEADS * SB_DH), sb_w_out[0].astype(BF16), h2, row(ffn_norm[1]),
                   ffn_w_gu[1].astype(BF16), ffn_w_down[1].astype(BF16), tm=tm_ffn)
    return h2.reshape(b, s, d)
```

```python
import functools

import jax
import jax.numpy as jnp
from jax import lax
from jax.experimental import pallas as pl
from jax.experimental.pallas import tpu as pltpu

F32 = jnp.float32
BF16 = jnp.bfloat16
EPS = 1e-6
LOG2E = 1.4426950408889634

LANES = 128
GDN_HEADS = 8
GDN_D = 128
CHUNK = 64
PAIR = 2 * CHUNK
CONV_WIDTH = 4
CONV_HALO = 8
SB_HEADS = 8
SB_DH = 128
VMEM_LIMIT = 56 * 1024 * 1024
ROW_CHUNK = 512
COL_CHUNK = 1024
FF_CHUNK = 256
DEAD_LOG2 = 160.0


def _dot(a, b):
    return jnp.dot(a, b, preferred_element_type=F32)


def _dot_nt(a, b):
    return lax.dot_general(a, b, (((1,), (1,)), ((), ())), preferred_element_type=F32)


def _silu(x):
    half = 0.5 * x
    return half + half * jnp.tanh(half)


def _softplus(x):
    return jnp.maximum(x, 0.0) + jnp.log(1.0 + jnp.exp(-jnp.abs(x)))


def _split_bf16(x):
    hi = x.astype(BF16)
    lo = (x - hi.astype(F32)).astype(BF16)
    return hi, lo


def _rms_scale(x):
    return lax.rsqrt(jnp.mean(x * x, axis=-1, keepdims=True) + EPS)


def _resident(shape):
    return pl.BlockSpec(shape, lambda *_: (0,) * len(shape), pipeline_mode=pl.Buffered(1))


def _inproj_kernel(x_ref, g_ref, w_ref, wba_ref, y_ref, ba_ref):
    tm, n = y_ref.shape
    for r0 in range(0, tm, ROW_CHUNK):
        rows = slice(r0, r0 + ROW_CHUNK)
        x = x_ref[rows, :]
        xh, xl = _split_bf16(x * _rms_scale(x) * g_ref[...])
        both = _dot(xh, wba_ref[...])
        ba_ref[rows, :] = both[:, :LANES] + both[:, LANES:] + _dot(xl, wba_ref[:, :LANES])
        for c0 in range(0, n, COL_CHUNK):
            cols = slice(c0, c0 + COL_CHUNK)
            y_ref[rows, cols] = _dot(xh, w_ref[:, cols]).astype(y_ref.dtype)


def _gdn_inproj(x2, gain, w_main, wba, *, tm):
    t, d = x2.shape
    n = w_main.shape[1]
    return pl.pallas_call(
        _inproj_kernel,
        out_shape=(jax.ShapeDtypeStruct((t, n), BF16),
                   jax.ShapeDtypeStruct((t, LANES), F32)),
        grid=(t // tm,),
        in_specs=[pl.BlockSpec((tm, d), lambda i: (i, 0)),
                  _resident((1, d)), _resident((d, n)), _resident((d, 2 * LANES))],
        out_specs=(pl.BlockSpec((tm, n), lambda i: (i, 0)),
                   pl.BlockSpec((tm, LANES), lambda i: (i, 0))),
        compiler_params=pltpu.CompilerParams(
            dimension_semantics=("parallel",), vmem_limit_bytes=VMEM_LIMIT),
        name="gdn_inproj",
    )(x2, gain, w_main, wba)


def _gdn_kernel(q_ref, k_ref, v_ref, z_ref, ba_ref, cwq_ref, cwk_ref, cwv_ref,
                alog_ref, dtb_ref, og_ref, cum_ref, o_ref, state_ref, ext_ref, *, blk, hb):
    hg = pl.program_id(1)
    width = hb * GDN_D
    npair = blk // PAIR

    @pl.when(pl.program_id(2) == 0)
    def _():
        state_ref[...] = jnp.zeros_like(state_ref)
        ext_ref[:, 0:CONV_HALO, :] = jnp.zeros((3, CONV_HALO, width), F32)

    def conv_silu(idx, x_ref, cw_ref):
        ext_ref[idx, CONV_HALO:CONV_HALO + blk, :] = x_ref[...].astype(F32)
        y = cw_ref[CONV_WIDTH - 1:CONV_WIDTH, :] * ext_ref[idx, CONV_HALO:CONV_HALO + blk, :]
        for j in range(CONV_WIDTH - 1):
            off = CONV_HALO - (CONV_WIDTH - 1) + j
            y = y + cw_ref[j:j + 1, :] * ext_ref[idx, off:off + blk, :]
        ext_ref[idx, 0:CONV_HALO, :] = ext_ref[idx, blk:blk + CONV_HALO, :]
        return _silu(y)

    q_all = conv_silu(0, q_ref, cwq_ref)
    k_all = conv_silu(1, k_ref, cwk_ref)
    v_all = conv_silu(2, v_ref, cwv_ref)

    ba = ba_ref[...]
    lane = lax.broadcasted_iota(jnp.int32, ba.shape, 1)
    beta_all = 1.0 / (1.0 + jnp.exp(-ba))
    g_hi, g_lo = _split_bf16(-jnp.exp(alog_ref[...]) * _softplus(ba + dtb_ref[...]))
    gc_all = _dot(cum_ref[...], g_hi) + _dot(cum_ref[...], g_lo)

    row = lax.broadcasted_iota(jnp.int32, (PAIR, PAIR), 0)
    col = lax.broadcasted_iota(jnp.int32, (PAIR, PAIR), 1)
    col2 = lax.broadcasted_iota(jnp.int32, (2 * PAIR, PAIR), 1)
    same = (row & -CHUNK) == (col & -CHUNK)
    incl = same & (row >= col)
    strict = same & (row > col)

    probs = []
    qs, ks, vs, betas, gcs = [], [], [], [], []
    for hh in range(hb):
        head = hg * hb + hh
        c0 = hh * GDN_D
        qh, kh, vh = q_all[:, c0:c0 + GDN_D], k_all[:, c0:c0 + GDN_D], v_all[:, c0:c0 + GDN_D]
        qh = qh * (lax.rsqrt(jnp.sum(qh * qh, axis=-1, keepdims=True) + EPS) * (GDN_D ** -0.5))
        kh = kh * lax.rsqrt(jnp.sum(kh * kh, axis=-1, keepdims=True) + EPS)
        beta = jnp.sum(jnp.where(lane == head, beta_all, 0.0), axis=-1, keepdims=True)
        gc = jnp.sum(jnp.where(lane == head + GDN_HEADS, gc_all, 0.0), axis=-1, keepdims=True)
        for p in range(npair):
            r0 = p * PAIR
            probs.append((hh, r0))
            qs.append(qh[r0:r0 + PAIR])
            ks.append(kh[r0:r0 + PAIR])
            vs.append(vh[r0:r0 + PAIR])
            betas.append(beta[r0:r0 + PAIR])
            gcs.append(jnp.broadcast_to(gc[r0:r0 + PAIR], (PAIR, PAIR)))
    n = len(probs)
    idx = range(n)

    decays = [jnp.exp(jnp.where(incl, gcs[i] - gcs[i].T, -jnp.inf)) for i in idx]
    kbs = [ks[i] * betas[i] for i in idx]
    kbf = [ks[i].astype(BF16) for i in idx]
    a_lows = [jnp.where(strict, _dot_nt(kbs[i].astype(BF16), kbf[i]) * decays[i], 0.0) for i in idx]
    attns = [(_dot_nt(qs[i].astype(BF16), kbf[i]) * decays[i]).astype(BF16) for i in idx]

    eye = jnp.where(row == col, 1.0, 0.0)
    ts = [eye - jnp.where((row - col == 1) & ((row & 1) == 1), a_lows[i], 0.0) for i in idx]
    s = 2
    while s < CHUNK:
        sub = ((row & -(2 * s)) == (col & -(2 * s))) & ((row & s) != 0) & ((col & s) == 0)
        tbs = [ts[i].astype(BF16) for i in idx]
        xs = [_dot(jnp.where(sub, a_lows[i], 0.0).astype(BF16), tbs[i]).astype(BF16) for i in idx]
        ts = [ts[i] - _dot(tbs[i], xs[i]) for i in idx]
        s *= 2

    egs = [jnp.exp(gcs[i]) for i in idx]
    rhs = [jnp.concatenate([vs[i] * betas[i], kbs[i] * egs[i]], axis=-1).astype(BF16) for i in idx]
    sols = [_dot(ts[i].astype(BF16), rhs[i]) for i in idx]
    aws = [_dot(attns[i], sols[i].astype(BF16)) for i in idx]
    qps = [(qs[i] * egs[i] - aws[i][:, GDN_D:]).astype(BF16) for i in idx]
    g_last = [[gcs[i][c * CHUNK + CHUNK - 1:(c + 1) * CHUNK, :] for c in range(2)] for i in idx]
    kds = [(ks[i] * jnp.exp(jnp.where(row < CHUNK, g_last[i][0], g_last[i][1]) - gcs[i])).astype(BF16)
           for i in idx]
    sol_ts = [sols[i].T for i in idx]
    pns = [[_dot(jnp.where((col2 & CHUNK) == c * CHUNK, sol_ts[i], 0.0).astype(BF16), kds[i])
            for c in range(2)] for i in idx]

    states = [state_ref[hh] for hh in range(hb)]
    inters = [[None, None] for _ in idx]
    for p in range(npair):
        for c in range(2):
            for hh in range(hb):
                i = hh * npair + p
                st = states[hh]
                st_bf = st.astype(BF16)
                inters[i][c] = _dot_nt(qps[i][c * CHUNK:(c + 1) * CHUNK], st_bf)
                pn = pns[i][c]
                states[hh] = (st * jnp.exp(g_last[i][c]) - _dot(st_bf, pn[GDN_D:].astype(BF16))
                              + pn[:GDN_D])
    for hh in range(hb):
        state_ref[hh] = states[hh]

    og = og_ref[...]
    for i in idx:
        hh, r0 = probs[i]
        c0 = hh * GDN_D
        o = jnp.concatenate(inters[i], axis=0) + aws[i][:, :GDN_D]
        zg = z_ref[r0:r0 + PAIR, c0:c0 + GDN_D].astype(F32)
        o = o * _rms_scale(o) * og * _silu(zg)
        o_ref[r0:r0 + PAIR, c0:c0 + GDN_D] = o.astype(o_ref.dtype)


def _gdn_mixer(proj3, ba3, conv_w8, alog_pad, dtb_pad, o_gain, cum_mat, *, blk, hb):
    b, s, _ = proj3.shape
    ng = GDN_HEADS // hb
    width = hb * GDN_D
    tok = lambda off: pl.BlockSpec((None, blk, width), lambda bi, gi, li: (bi, li, gi + off))
    cw = lambda off: pl.BlockSpec((8, width), lambda bi, gi, li: (0, gi + off))
    vec = pl.BlockSpec((1, LANES), lambda bi, gi, li: (0, 0))
    return pl.pallas_call(
        functools.partial(_gdn_kernel, blk=blk, hb=hb),
        out_shape=jax.ShapeDtypeStruct((b, s, GDN_HEADS * GDN_D), BF16),
        grid=(b, ng, s // blk),
        in_specs=[tok(0), tok(ng), tok(2 * ng), tok(3 * ng),
                  pl.BlockSpec((None, blk, LANES), lambda bi, gi, li: (bi, li, 0)),
                  cw(0), cw(ng), cw(2 * ng), vec, vec, vec,
                  pl.BlockSpec((blk, blk), lambda bi, gi, li: (0, 0))],
        out_specs=pl.BlockSpec((None, blk, width), lambda bi, gi, li: (bi, li, gi)),
        scratch_shapes=[pltpu.VMEM((hb, GDN_D, GDN_D), F32),
                        pltpu.VMEM((3, blk + CONV_HALO, width), F32)],
        compiler_params=pltpu.CompilerParams(
            dimension_semantics=("parallel", "parallel", "arbitrary"), vmem_limit_bytes=VMEM_LIMIT),
        name="gdn_mixer",
    )(proj3, proj3, proj3, proj3, ba3, conv_w8, conv_w8, conv_w8, alog_pad, dtb_pad, o_gain, cum_mat)


def _proj_ffn_kernel(a_ref, wo_ref, res_ref, g_ref, wgu_ref, wd_ref, o_ref, mid_ref):
    f = wd_ref.shape[0]
    h1 = res_ref[...] + _dot(a_ref[...], wo_ref[...])
    o_ref[...] = h1
    xn = (h1 * _rms_scale(h1) * g_ref[...]).astype(BF16)
    for c0 in range(0, f, FF_CHUNK):
        gate = _dot(xn, wgu_ref[:, c0:c0 + FF_CHUNK])
        up = _dot(xn, wgu_ref[:, f + c0:f + c0 + FF_CHUNK])
        mid_ref[:, c0:c0 + FF_CHUNK] = (_silu(gate) * up).astype(BF16)
    o_ref[...] += _dot(mid_ref[...], wd_ref[...])


def _proj_ffn(a2, w_out, res2, gain, w_gu_all, w_down_all, layer, *, tm):
    t, d = res2.shape
    ka = a2.shape[1]
    f = w_down_all.shape[1]
    layer_weights = lambda shape: pl.BlockSpec((None,) + shape, lambda i: (layer, 0, 0),
                                               pipeline_mode=pl.Buffered(1))
    return pl.pallas_call(
        _proj_ffn_kernel,
        out_shape=jax.ShapeDtypeStruct((t, d), F32),
        grid=(t // tm,),
        in_specs=[pl.BlockSpec((tm, ka), lambda i: (i, 0)),
                  _resident((ka, d)),
                  pl.BlockSpec((tm, d), lambda i: (i, 0)),
                  _resident((1, d)), layer_weights((d, 2 * f)), layer_weights((f, d))],
        out_specs=pl.BlockSpec((tm, d), lambda i: (i, 0)),
        scratch_shapes=[pltpu.VMEM((tm, f), BF16)],
        compiler_params=pltpu.CompilerParams(
            dimension_semantics=("parallel",), vmem_limit_bytes=VMEM_LIMIT),
        name="proj_ffn",
    )(a2, w_out, res2, gain, w_gu_all, w_down_all)


def _sb_qkv_kernel(h_ref, gq_ref, gkv_ref, wq_ref, wkv_ref, qg_ref, kg_ref, o_ref):
    tm = h_ref.shape[0]
    nq = SB_HEADS * SB_DH

    def head_norm(rows, c0, y, gain):
        for c in range(0, y.shape[1], SB_DH):
            yc = y[:, c:c + SB_DH]
            o_ref[rows, c0 + c:c0 + c + SB_DH] = (yc * _rms_scale(yc) * gain).astype(o_ref.dtype)

    for r0 in range(0, tm, ROW_CHUNK):
        rows = slice(r0, r0 + ROW_CHUNK)
        x = h_ref[rows, :]
        xs = x * _rms_scale(x)
        xq = (xs * gq_ref[...]).astype(BF16)
        xkv = (xs * gkv_ref[...]).astype(BF16)
        head_norm(rows, 0, _dot(xq, wq_ref[...]), qg_ref[...] * (SB_DH ** -0.5 * LOG2E))
        head_norm(rows, nq, _dot(xkv, wkv_ref[:, 0:nq]), kg_ref[...])
        o_ref[rows, 2 * nq:3 * nq] = _dot(xkv, wkv_ref[:, nq:2 * nq]).astype(o_ref.dtype)


def _sb_qkv(h2, gain_q, gain_kv, w_q, w_kv, q_gain, k_gain, *, tm):
    t, d = h2.shape
    n = w_q.shape[1] + w_kv.shape[1]
    return pl.pallas_call(
        _sb_qkv_kernel,
        out_shape=jax.ShapeDtypeStruct((t, n), BF16),
        grid=(t // tm,),
        in_specs=[pl.BlockSpec((tm, d), lambda i: (i, 0)), _resident((1, d)), _resident((1, d)),
                  _resident(w_q.shape), _resident(w_kv.shape),
                  _resident((1, SB_DH)), _resident((1, SB_DH))],
        out_specs=pl.BlockSpec((tm, n), lambda i: (i, 0)),
        compiler_params=pltpu.CompilerParams(
            dimension_semantics=("parallel",), vmem_limit_bytes=VMEM_LIMIT),
        name="sb_qkv",
    )(h2, gain_q, gain_kv, w_q, w_kv, q_gain, k_gain)


def _sb_attn_kernel(q_ref, k_ref, v_ref, suf_ref, o_ref, *, tq, tk, hb):
    nd = tq // tk
    qi = pl.program_id(2)
    suf = suf_ref[...]
    ahead = (lax.broadcasted_iota(jnp.int32, (tq, tk), 0)
             - lax.broadcasted_iota(jnp.int32, (tq, tk), 1))
    heads = [slice(hh * SB_DH, (hh + 1) * SB_DH) for hh in range(hb)]
    qs = [q_ref[:, hs] for hs in heads]

    def scores(hh, j):
        z = _dot_nt(qs[hh], k_ref[pl.ds(pl.multiple_of(j * tk, tk), tk), heads[hh]])
        return z, jnp.maximum(z, 0.0) + jnp.log2(1.0 + jnp.exp2(-jnp.abs(z)))

    def absorb(hh, j, z, sp, rest, acc, causal):
        fail = sp if causal is None else jnp.where(causal, sp, 0.0)
        after = _dot(fail.astype(BF16), suf) + rest
        att = jnp.exp2((z - sp) - after)
        if causal is not None:
            att = jnp.where(causal, att, 0.0)
        vb = v_ref[pl.ds(pl.multiple_of(j * tk, tk), tk), heads[hh]]
        return rest + jnp.sum(fail, axis=-1, keepdims=True), acc + _dot(att.astype(BF16), vb)

    diag = list(range(nd - 1, -1, -1))
    has_prev = qi > 0
    prev = jnp.maximum(nd * qi - 1, 0)
    zs = [[scores(hh, nd * qi + d) for d in diag] + [scores(hh, prev)] for hh in range(hb)]
    carry = []
    for hh in range(hb):
        rest, acc = jnp.zeros((tq, 1), F32), jnp.zeros((tq, SB_DH), F32)
        for i, d in enumerate(diag):
            rest, acc = absorb(hh, nd * qi + d, *zs[hh][i], rest, acc, ahead > d * tk)
        rest, acc = absorb(hh, prev, *zs[hh][nd], rest, acc, has_prev)
        carry += [rest, acc]

    def low_water(carry):
        return functools.reduce(jnp.minimum, [jnp.min(carry[2 * hh]) for hh in range(hb)])

    def cond(state):
        it, low = state[0], state[1]
        return (it < nd * qi - 1) & (low < DEAD_LOG2)

    def body(state):
        it, carry = state[0], list(state[2:])
        j = nd * qi - 2 - it
        zs = [scores(hh, j) for hh in range(hb)]
        for hh in range(hb):
            carry[2 * hh], carry[2 * hh + 1] = absorb(hh, j, *zs[hh], carry[2 * hh], carry[2 * hh + 1], None)
        return (it + 1, low_water(carry), *carry)

    final = lax.while_loop(cond, body, (jnp.int32(0), low_water(carry), *carry))
    for hh in range(hb):
        o_ref[:, heads[hh]] = final[3 + 2 * hh].astype(o_ref.dtype)


def _sb_attn(qkv3, suffix_mat, *, tq, tk, hb):
    b, s, _ = qkv3.shape
    ng = SB_HEADS // hb
    width = hb * SB_DH
    return pl.pallas_call(
        functools.partial(_sb_attn_kernel, tq=tq, tk=tk, hb=hb),
        out_shape=jax.ShapeDtypeStruct((b, s, SB_HEADS * SB_DH), BF16),
        grid=(b, ng, s // tq),
        in_specs=[pl.BlockSpec((None, tq, width), lambda bi, gi, qi: (bi, qi, gi)),
                  pl.BlockSpec((None, s, width), lambda bi, gi, qi: (bi, 0, gi + ng)),
                  pl.BlockSpec((None, s, width), lambda bi, gi, qi: (bi, 0, gi + 2 * ng)),
                  pl.BlockSpec((tk, tk), lambda bi, gi, qi: (0, 0))],
        out_specs=pl.BlockSpec((None, tq, width), lambda bi, gi, qi: (bi, qi, gi)),
        compiler_params=pltpu.CompilerParams(
            dimension_semantics=("parallel", "parallel", "arbitrary"), vmem_limit_bytes=VMEM_LIMIT),
        name="sb_attn",
    )(qkv3, qkv3, qkv3, suffix_mat)


def _pick(n, prefs):
    for p in prefs:
        if n % p == 0:
            return p
    return n


def kernel(x, attn_norm, ffn_norm, gdn_w_in, gdn_conv_w, gdn_a_log, gdn_dt_bias, gdn_o_gain, gdn_w_out,
           kv_norm, w_kv, k_gain, sb_w_q, sb_q_gain, sb_w_out, ffn_w_gu, ffn_w_down):
    b, s, d = x.shape
    t = b * s
    hd = GDN_HEADS * GDN_D
    n_main = 4 * hd
    tm = _pick(t, (1024, ROW_CHUNK))
    tm_ffn = ROW_CHUNK
    gdn_blk = _pick(s, (256, 128))
    tk = _pick(s, (256, 128))

    row = lambda vec: vec.reshape(1, -1).astype(F32)
    x2 = x.reshape(t, d)

    w_in = gdn_w_in[0]
    w_ba = jnp.pad(w_in[:, n_main:], ((0, 0), (0, LANES - 2 * GDN_HEADS)))
    wba_hi = w_ba.astype(BF16)
    wba_lo = (w_ba - wba_hi.astype(F32)).astype(BF16)
    proj, ba = _gdn_inproj(x2, row(attn_norm[0]), w_in[:, :n_main].astype(BF16),
                           jnp.concatenate([wba_hi, wba_lo], axis=1), tm=tm)
    lane_pad = lambda vec: jnp.pad(vec.astype(F32), (GDN_HEADS, LANES - 2 * GDN_HEADS)).reshape(1, LANES)
    conv_w8 = jnp.pad(gdn_conv_w[0].astype(F32), ((0, 8 - CONV_WIDTH), (0, 0)))
    tok = jnp.arange(gdn_blk)
    cum_mat = ((tok[:, None] // CHUNK == tok[None, :] // CHUNK) & (tok[:, None] >= tok[None, :])).astype(BF16)
    o_gdn = _gdn_mixer(proj.reshape(b, s, n_main), ba.reshape(b, s, LANES), conv_w8,
                       lane_pad(gdn_a_log[0]), lane_pad(gdn_dt_bias[0]), row(gdn_o_gain[0]), cum_mat,
                       blk=gdn_blk, hb=8)
    w_gu_all, w_down_all = ffn_w_gu.astype(BF16), ffn_w_down.astype(BF16)
    h2 = _proj_ffn(o_gdn.reshape(t, hd), gdn_w_out[0].astype(BF16), x2, row(ffn_norm[0]),
                   w_gu_all, w_down_all, 0, tm=tm_ffn)

    qkv = _sb_qkv(h2, row(attn_norm[1]), row(kv_norm), sb_w_q[0].astype(BF16), w_kv.astype(BF16),
                  row(sb_q_gain[0]), row(k_gain), tm=tm)
    idx = jnp.arange(tk)
    suffix_mat = (idx[:, None] > idx[None, :]).astype(BF16)
    o_sb = _sb_attn(qkv.reshape(b, s, 3 * SB_HEADS * SB_DH), suffix_mat, tq=tk, tk=tk, hb=4)
    h2 = _proj_ffn(o_sb.reshape(t, SB_HEADS * SB_DH), sb_w_out[0].astype(BF16), h2, row(ffn_norm[1]),
                   w_gu_all, w_down_all, 1, tm=tm_ffn)
    return h2.reshape(b, s, d)
```

```python
import functools

import jax
import jax.numpy as jnp
from jax import lax
from jax.experimental import pallas as pl
from jax.experimental.pallas import tpu as pltpu

F32 = jnp.float32
BF16 = jnp.bfloat16
EPS = 1e-6
LOG2E = 1.4426950408889634

LANES = 128
GDN_HEADS = 8
GDN_D = 128
CHUNK = 64
PAIR = 2 * CHUNK
CONV_WIDTH = 4
CONV_HALO = 8
SB_HEADS = 8
SB_DH = 128
VMEM_LIMIT = 56 * 1024 * 1024
ROW_CHUNK = 512
COL_CHUNK = 1024
FF_CHUNK = 256
DEAD_LOG2 = 160.0


def _dot(a, b):
    return jnp.dot(a, b, preferred_element_type=F32)


def _dot_nt(a, b):
    return lax.dot_general(a, b, (((1,), (1,)), ((), ())), preferred_element_type=F32)


def _silu(x):
    half = 0.5 * x
    return half + half * jnp.tanh(half)


def _softplus(x):
    return jnp.maximum(x, 0.0) + jnp.log(1.0 + jnp.exp(-jnp.abs(x)))


def _split_bf16(x):
    hi = x.astype(BF16)
    lo = (x - hi.astype(F32)).astype(BF16)
    return hi, lo


def _rms_scale(x):
    return lax.rsqrt(jnp.mean(x * x, axis=-1, keepdims=True) + EPS)


def _resident(shape):
    return pl.BlockSpec(shape, lambda *_: (0,) * len(shape), pipeline_mode=pl.Buffered(1))


def _inproj_kernel(x_ref, g_ref, w_ref, wba_ref, y_ref, ba_ref):
    tm, n = y_ref.shape
    for r0 in range(0, tm, ROW_CHUNK):
        rows = slice(r0, r0 + ROW_CHUNK)
        x = x_ref[rows, :]
        xh, xl = _split_bf16(x * _rms_scale(x) * g_ref[...])
        both = _dot(xh, wba_ref[...])
        ba_ref[rows, :] = both[:, :LANES] + both[:, LANES:] + _dot(xl, wba_ref[:, :LANES])
        for c0 in range(0, n, COL_CHUNK):
            cols = slice(c0, c0 + COL_CHUNK)
            y_ref[rows, cols] = _dot(xh, w_ref[:, cols]).astype(y_ref.dtype)


def _gdn_inproj(x2, gain, w_main, wba, *, n, tm):
    t, d = x2.shape
    return pl.pallas_call(
        _inproj_kernel,
        out_shape=(jax.ShapeDtypeStruct((t, n), BF16),
                   jax.ShapeDtypeStruct((t, LANES), F32)),
        grid=(t // tm,),
        in_specs=[pl.BlockSpec((tm, d), lambda i: (i, 0)),
                  _resident((1, d)), _resident(w_main.shape), _resident((d, 2 * LANES))],
        out_specs=(pl.BlockSpec((tm, n), lambda i: (i, 0)),
                   pl.BlockSpec((tm, LANES), lambda i: (i, 0))),
        compiler_params=pltpu.CompilerParams(
            dimension_semantics=("parallel",), vmem_limit_bytes=VMEM_LIMIT),
        name="gdn_inproj",
    )(x2, gain, w_main, wba)


def _gdn_kernel(q_ref, k_ref, v_ref, z_ref, ba_ref, cwq_ref, cwk_ref, cwv_ref,
                alog_ref, dtb_ref, og_ref, cum_ref, o_ref, state_ref, ext_ref, *, blk, hb):
    hg = pl.program_id(1)
    width = hb * GDN_D
    npair = blk // PAIR

    @pl.when(pl.program_id(2) == 0)
    def _():
        state_ref[...] = jnp.zeros_like(state_ref)
        ext_ref[...] = jnp.zeros_like(ext_ref)

    def conv_silu(idx, x_ref, cw_ref):
        x = x_ref[...].astype(F32)
        first = jnp.concatenate([ext_ref[idx], x[0:CONV_HALO]], axis=0)
        y = cw_ref[CONV_WIDTH - 1:CONV_WIDTH, :] * x
        for j in range(CONV_WIDTH - 1):
            delay = CONV_WIDTH - 1 - j
            head = pltpu.roll(first, delay, axis=0)[CONV_HALO:]
            delayed = jnp.concatenate([head, pltpu.roll(x, delay, axis=0)[CONV_HALO:]], axis=0)
            y = y + cw_ref[j:j + 1, :] * delayed
        ext_ref[idx] = x[blk - CONV_HALO:blk]
        return _silu(y)

    q_all = conv_silu(0, q_ref, cwq_ref)
    k_all = conv_silu(1, k_ref, cwk_ref)
    v_all = conv_silu(2, v_ref, cwv_ref)

    ba = ba_ref[...]
    lane = lax.broadcasted_iota(jnp.int32, ba.shape, 1)
    beta_all = 1.0 / (1.0 + jnp.exp(-ba))
    g_hi, g_lo = _split_bf16(-jnp.exp(alog_ref[...]) * _softplus(ba + dtb_ref[...]))
    gc_all = _dot(cum_ref[...], g_hi) + _dot(cum_ref[...], g_lo)

    row = lax.broadcasted_iota(jnp.int32, (PAIR, PAIR), 0)
    col = lax.broadcasted_iota(jnp.int32, (PAIR, PAIR), 1)
    col2 = lax.broadcasted_iota(jnp.int32, (2 * PAIR, PAIR), 1)
    same = (row & -CHUNK) == (col & -CHUNK)
    incl = same & (row >= col)
    strict = same & (row > col)

    probs = []
    qs, ks, vs, betas, gcs = [], [], [], [], []
    for hh in range(hb):
        head = hg * hb + hh
        c0 = hh * GDN_D
        qh, kh, vh = q_all[:, c0:c0 + GDN_D], k_all[:, c0:c0 + GDN_D], v_all[:, c0:c0 + GDN_D]
        qh = qh * (lax.rsqrt(jnp.sum(qh * qh, axis=-1, keepdims=True) + EPS) * (GDN_D ** -0.5))
        kh = kh * lax.rsqrt(jnp.sum(kh * kh, axis=-1, keepdims=True) + EPS)
        beta = jnp.sum(jnp.where(lane == head, beta_all, 0.0), axis=-1, keepdims=True)
        gc = jnp.sum(jnp.where(lane == head + GDN_HEADS, gc_all, 0.0), axis=-1, keepdims=True)
        for p in range(npair):
            r0 = p * PAIR
            probs.append((hh, r0))
            qs.append(qh[r0:r0 + PAIR])
            ks.append(kh[r0:r0 + PAIR])
            vs.append(vh[r0:r0 + PAIR])
            betas.append(beta[r0:r0 + PAIR])
            gcs.append(jnp.broadcast_to(gc[r0:r0 + PAIR], (PAIR, PAIR)))
    n = len(probs)
    idx = range(n)

    decays = [jnp.exp(jnp.where(incl, gcs[i] - gcs[i].T, -jnp.inf)) for i in idx]
    kbs = [ks[i] * betas[i] for i in idx]
    kbf = [ks[i].astype(BF16) for i in idx]
    a_lows = [jnp.where(strict, _dot_nt(kbs[i].astype(BF16), kbf[i]) * decays[i], 0.0) for i in idx]
    attns = [(_dot_nt(qs[i].astype(BF16), kbf[i]) * decays[i]).astype(BF16) for i in idx]

    eye = jnp.where(row == col, 1.0, 0.0)
    ts = [eye - jnp.where((row - col == 1) & ((row & 1) == 1), a_lows[i], 0.0) for i in idx]
    s = 2
    while s < CHUNK:
        sub = ((row & -(2 * s)) == (col & -(2 * s))) & ((row & s) != 0) & ((col & s) == 0)
        tbs = [ts[i].astype(BF16) for i in idx]
        xs = [_dot(jnp.where(sub, a_lows[i], 0.0).astype(BF16), tbs[i]).astype(BF16) for i in idx]
        ts = [ts[i] - _dot(tbs[i], xs[i]) for i in idx]
        s *= 2

    egs = [jnp.exp(gcs[i]) for i in idx]
    rhs = [jnp.concatenate([vs[i] * betas[i], kbs[i] * egs[i]], axis=-1).astype(BF16) for i in idx]
    sols = [_dot(ts[i].astype(BF16), rhs[i]) for i in idx]
    aws = [_dot(attns[i], sols[i].astype(BF16)) for i in idx]
    qps = [(qs[i] * egs[i] - aws[i][:, GDN_D:]).astype(BF16) for i in idx]
    g_last = [[gcs[i][c * CHUNK + CHUNK - 1:(c + 1) * CHUNK, :] for c in range(2)] for i in idx]
    kds = [(ks[i] * jnp.exp(jnp.where(row < CHUNK, g_last[i][0], g_last[i][1]) - gcs[i])).astype(BF16)
           for i in idx]
    sol_ts = [sols[i].T for i in idx]
    pns = [[_dot(jnp.where((col2 & CHUNK) == c * CHUNK, sol_ts[i], 0.0).astype(BF16), kds[i])
            for c in range(2)] for i in idx]

    states = [state_ref[hh] for hh in range(hb)]
    inters = [[None, None] for _ in idx]
    for p in range(npair):
        for c in range(2):
            for hh in range(hb):
                i = hh * npair + p
                st = states[hh]
                st_bf = st.astype(BF16)
                inters[i][c] = _dot_nt(qps[i][c * CHUNK:(c + 1) * CHUNK], st_bf)
                pn = pns[i][c]
                states[hh] = (st * jnp.exp(g_last[i][c]) - _dot(st_bf, pn[GDN_D:].astype(BF16))
                              + pn[:GDN_D])
    for hh in range(hb):
        state_ref[hh] = states[hh]

    og = og_ref[...]
    for i in idx:
        hh, r0 = probs[i]
        c0 = hh * GDN_D
        o = jnp.concatenate(inters[i], axis=0) + aws[i][:, :GDN_D]
        zg = z_ref[r0:r0 + PAIR, c0:c0 + GDN_D].astype(F32)
        o = o * _rms_scale(o) * og * _silu(zg)
        o_ref[r0:r0 + PAIR, c0:c0 + GDN_D] = o.astype(o_ref.dtype)


def _gdn_mixer(proj3, ba3, conv_w8, alog_pad, dtb_pad, o_gain, cum_mat, *, blk, hb):
    b, s, _ = proj3.shape
    ng = GDN_HEADS // hb
    width = hb * GDN_D
    tok = lambda off: pl.BlockSpec((None, blk, width), lambda bi, gi, li: (bi, li, gi + off))
    cw = lambda off: pl.BlockSpec((8, width), lambda bi, gi, li: (0, gi + off))
    vec = pl.BlockSpec((1, LANES), lambda bi, gi, li: (0, 0))
    return pl.pallas_call(
        functools.partial(_gdn_kernel, blk=blk, hb=hb),
        out_shape=jax.ShapeDtypeStruct((b, s, GDN_HEADS * GDN_D), BF16),
        grid=(b, ng, s // blk),
        in_specs=[tok(0), tok(ng), tok(2 * ng), tok(3 * ng),
                  pl.BlockSpec((None, blk, LANES), lambda bi, gi, li: (bi, li, 0)),
                  cw(0), cw(ng), cw(2 * ng), vec, vec, vec,
                  pl.BlockSpec((blk, blk), lambda bi, gi, li: (0, 0))],
        out_specs=pl.BlockSpec((None, blk, width), lambda bi, gi, li: (bi, li, gi)),
        scratch_shapes=[pltpu.VMEM((hb, GDN_D, GDN_D), F32),
                        pltpu.VMEM((3, CONV_HALO, width), F32)],
        compiler_params=pltpu.CompilerParams(
            dimension_semantics=("parallel", "parallel", "arbitrary"), vmem_limit_bytes=VMEM_LIMIT),
        name="gdn_mixer",
    )(proj3, proj3, proj3, proj3, ba3, conv_w8, conv_w8, conv_w8, alog_pad, dtb_pad, o_gain, cum_mat)


def _proj_ffn_kernel(a_ref, wo_ref, res_ref, g_ref, wgu_ref, wd_ref, o_ref, mid_ref):
    f = wd_ref.shape[0]
    h1 = res_ref[...] + _dot(a_ref[...], wo_ref[...])
    o_ref[...] = h1
    xn = (h1 * _rms_scale(h1) * g_ref[...]).astype(BF16)
    for c0 in range(0, f, FF_CHUNK):
        gate = _dot(xn, wgu_ref[:, c0:c0 + FF_CHUNK])
        up = _dot(xn, wgu_ref[:, f + c0:f + c0 + FF_CHUNK])
        mid_ref[:, c0:c0 + FF_CHUNK] = (_silu(gate) * up).astype(BF16)
    o_ref[...] += _dot(mid_ref[...], wd_ref[...])


def _proj_ffn(a2, w_out, res2, gain, w_gu_all, w_down_all, layer, *, tm):
    t, d = res2.shape
    ka = a2.shape[1]
    f = w_down_all.shape[1]
    layer_weights = lambda shape: pl.BlockSpec((None,) + shape, lambda i: (layer, 0, 0),
                                               pipeline_mode=pl.Buffered(1))
    return pl.pallas_call(
        _proj_ffn_kernel,
        out_shape=jax.ShapeDtypeStruct((t, d), F32),
        grid=(t // tm,),
        in_specs=[pl.BlockSpec((tm, ka), lambda i: (i, 0)),
                  _resident((ka, d)),
                  pl.BlockSpec((tm, d), lambda i: (i, 0)),
                  _resident((1, d)), layer_weights((d, 2 * f)), layer_weights((f, d))],
        out_specs=pl.BlockSpec((tm, d), lambda i: (i, 0)),
        scratch_shapes=[pltpu.VMEM((tm, f), BF16)],
        compiler_params=pltpu.CompilerParams(
            dimension_semantics=("parallel",), vmem_limit_bytes=VMEM_LIMIT),
        name="proj_ffn",
    )(a2, w_out, res2, gain, w_gu_all, w_down_all)


def _sb_qkv_kernel(h_ref, gq_ref, gkv_ref, wq_ref, wkv_ref, qg_ref, kg_ref, o_ref):
    tm = h_ref.shape[0]
    nq = SB_HEADS * SB_DH

    def head_norm(rows, c0, y, gain):
        for c in range(0, y.shape[1], SB_DH):
            yc = y[:, c:c + SB_DH]
            o_ref[rows, c0 + c:c0 + c + SB_DH] = (yc * _rms_scale(yc) * gain).astype(o_ref.dtype)

    for r0 in range(0, tm, ROW_CHUNK):
        rows = slice(r0, r0 + ROW_CHUNK)
        x = h_ref[rows, :]
        xs = x * _rms_scale(x)
        xq = (xs * gq_ref[...]).astype(BF16)
        xkv = (xs * gkv_ref[...]).astype(BF16)
        head_norm(rows, 0, _dot(xq, wq_ref[...]), qg_ref[...] * (SB_DH ** -0.5 * LOG2E))
        head_norm(rows, nq, _dot(xkv, wkv_ref[:, 0:nq]), kg_ref[...])
        o_ref[rows, 2 * nq:3 * nq] = _dot(xkv, wkv_ref[:, nq:2 * nq]).astype(o_ref.dtype)


def _sb_qkv(h2, gain_q, gain_kv, w_q, w_kv, q_gain, k_gain, *, tm):
    t, d = h2.shape
    n = w_q.shape[1] + w_kv.shape[1]
    return pl.pallas_call(
        _sb_qkv_kernel,
        out_shape=jax.ShapeDtypeStruct((t, n), BF16),
        grid=(t // tm,),
        in_specs=[pl.BlockSpec((tm, d), lambda i: (i, 0)), _resident((1, d)), _resident((1, d)),
                  _resident(w_q.shape), _resident(w_kv.shape),
                  _resident((1, SB_DH)), _resident((1, SB_DH))],
        out_specs=pl.BlockSpec((tm, n), lambda i: (i, 0)),
        compiler_params=pltpu.CompilerParams(
            dimension_semantics=("parallel",), vmem_limit_bytes=VMEM_LIMIT),
        name="sb_qkv",
    )(h2, gain_q, gain_kv, w_q, w_kv, q_gain, k_gain)


def _sb_attn_kernel(q_ref, k_ref, v_ref, suf_ref, o_ref, *, tq, tk, hb):
    nd = tq // tk
    qi = pl.program_id(2)
    suf = suf_ref[...]
    ahead = (lax.broadcasted_iota(jnp.int32, (tq, tk), 0)
             - lax.broadcasted_iota(jnp.int32, (tq, tk), 1))
    heads = [slice(hh * SB_DH, (hh + 1) * SB_DH) for hh in range(hb)]
    qs = [q_ref[:, hs] for hs in heads]

    def scores(hh, j):
        z = _dot_nt(qs[hh], k_ref[pl.ds(pl.multiple_of(j * tk, tk), tk), heads[hh]])
        return z, jnp.maximum(z, 0.0) + jnp.log2(1.0 + jnp.exp2(-jnp.abs(z)))

    def absorb(hh, j, z, sp, rest, acc, causal):
        fail = sp if causal is None else jnp.where(causal, sp, 0.0)
        after = _dot(fail.astype(BF16), suf) + rest
        att = jnp.exp2((z - sp) - after)
        if causal is not None:
            att = jnp.where(causal, att, 0.0)
        vb = v_ref[pl.ds(pl.multiple_of(j * tk, tk), tk), heads[hh]]
        return rest + jnp.sum(fail, axis=-1, keepdims=True), acc + _dot(att.astype(BF16), vb)

    diag = list(range(nd - 1, -1, -1))
    has_prev = qi > 0
    prev = jnp.maximum(nd * qi - 1, 0)
    zs = [[scores(hh, nd * qi + d) for d in diag] + [scores(hh, prev)] for hh in range(hb)]
    carry = []
    for hh in range(hb):
        rest, acc = jnp.zeros((tq, 1), F32), jnp.zeros((tq, SB_DH), F32)
        for i, d in enumerate(diag):
            rest, acc = absorb(hh, nd * qi + d, *zs[hh][i], rest, acc, ahead > d * tk)
        rest, acc = absorb(hh, prev, *zs[hh][nd], rest, acc, has_prev)
        carry += [rest, acc]

    def low_water(carry):
        return functools.reduce(jnp.minimum, [jnp.min(carry[2 * hh]) for hh in range(hb)])

    def cond(state):
        it, low = state[0], state[1]
        return (it < nd * qi - 1) & (low < DEAD_LOG2)

    def body(state):
        it, carry = state[0], list(state[2:])
        j = nd * qi - 2 - it
        zs = [scores(hh, j) for hh in range(hb)]
        for hh in range(hb):
            carry[2 * hh], carry[2 * hh + 1] = absorb(hh, j, *zs[hh], carry[2 * hh], carry[2 * hh + 1], None)
        return (it + 1, low_water(carry), *carry)

    final = lax.while_loop(cond, body, (jnp.int32(0), low_water(carry), *carry))
    for hh in range(hb):
        o_ref[:, heads[hh]] = final[3 + 2 * hh].astype(o_ref.dtype)


def _sb_attn(qkv3, suffix_mat, *, tq, tk, hb):
    b, s, _ = qkv3.shape
    ng = SB_HEADS // hb
    width = hb * SB_DH
    return pl.pallas_call(
        functools.partial(_sb_attn_kernel, tq=tq, tk=tk, hb=hb),
        out_shape=jax.ShapeDtypeStruct((b, s, SB_HEADS * SB_DH), BF16),
        grid=(b, ng, s // tq),
        in_specs=[pl.BlockSpec((None, tq, width), lambda bi, gi, qi: (bi, qi, gi)),
                  pl.BlockSpec((None, s, width), lambda bi, gi, qi: (bi, 0, gi + ng)),
                  pl.BlockSpec((None, s, width), lambda bi, gi, qi: (bi, 0, gi + 2 * ng)),
                  pl.BlockSpec((tk, tk), lambda bi, gi, qi: (0, 0))],
        out_specs=pl.BlockSpec((None, tq, width), lambda bi, gi, qi: (bi, qi, gi)),
        compiler_params=pltpu.CompilerParams(
            dimension_semantics=("parallel", "parallel", "arbitrary"), vmem_limit_bytes=VMEM_LIMIT),
        name="sb_attn",
    )(qkv3, qkv3, qkv3, suffix_mat)


def _pick(n, prefs):
    for p in prefs:
        if n % p == 0:
            return p
    return n


def kernel(x, attn_norm, ffn_norm, gdn_w_in, gdn_conv_w, gdn_a_log, gdn_dt_bias, gdn_o_gain, gdn_w_out,
           kv_norm, w_kv, k_gain, sb_w_q, sb_q_gain, sb_w_out, ffn_w_gu, ffn_w_down):
    b, s, d = x.shape
    t = b * s
    hd = GDN_HEADS * GDN_D
    n_main = 4 * hd
    tm = _pick(t, (1024, ROW_CHUNK))
    tm_ffn = ROW_CHUNK
    gdn_blk = _pick(s, (256, 128))
    tk = _pick(s, (256, 128))

    row = lambda vec: vec.reshape(1, -1).astype(F32)
    x2 = x.reshape(t, d)

    w_in = gdn_w_in[0]
    w_ba = jnp.pad(w_in[:, n_main:], ((0, 0), (0, LANES - 2 * GDN_HEADS)))
    wba_hi = w_ba.astype(BF16)
    wba_lo = (w_ba - wba_hi.astype(F32)).astype(BF16)
    proj, ba = _gdn_inproj(x2, row(attn_norm[0]), w_in.astype(BF16),
                           jnp.concatenate([wba_hi, wba_lo], axis=1), n=n_main, tm=tm)
    lane_pad = lambda vec: jnp.pad(vec.astype(F32), (GDN_HEADS, LANES - 2 * GDN_HEADS)).reshape(1, LANES)
    conv_w8 = jnp.pad(gdn_conv_w[0].astype(F32), ((0, 8 - CONV_WIDTH), (0, 0)))
    tok = jnp.arange(gdn_blk)
    cum_mat = ((tok[:, None] // CHUNK == tok[None, :] // CHUNK) & (tok[:, None] >= tok[None, :])).astype(BF16)
    o_gdn = _gdn_mixer(proj.reshape(b, s, n_main), ba.reshape(b, s, LANES), conv_w8,
                       lane_pad(gdn_a_log[0]), lane_pad(gdn_dt_bias[0]), row(gdn_o_gain[0]), cum_mat,
                       blk=gdn_blk, hb=8)
    w_gu_all, w_down_all = ffn_w_gu.astype(BF16), ffn_w_down.astype(BF16)
    h2 = _proj_ffn(o_gdn.reshape(t, hd), gdn_w_out[0].astype(BF16), x2, row(ffn_norm[0]),
                   w_gu_all, w_down_all, 0, tm=tm_ffn)

    qkv = _sb_qkv(h2, row(attn_norm[1]), row(kv_norm), sb_w_q[0].astype(BF16), w_kv.astype(BF16),
                  row(sb_q_gain[0]), row(k_gain), tm=tm)
    idx = jnp.arange(tk)
    suffix_mat = (idx[:, None] > idx[None, :]).astype(BF16)
    o_sb = _sb_attn(qkv.reshape(b, s, 3 * SB_HEADS * SB_DH), suffix_mat, tq=tk, tk=tk, hb=4)
    h2 = _proj_ffn(o_sb.reshape(t, SB_HEADS * SB_DH), sb_w_out[0].astype(BF16), h2, row(ffn_norm[1]),
                   w_gu_all, w_down_all, 1, tm=tm_ffn)
    return h2.reshape(b, s, d)
```
